```python
import math
import jax, jax.numpy as jnp
from jax import lax
import numpy as np

D_MODEL = 1024
BATCH = 8
SEQ = 4096
DEPTH = 2

D_FF = 2816
HEAD_DIM = 64
ROPE_DIM = HEAD_DIM // 4
ROPE_THETA = 500000.0
EPS = 1e-6
Q_BLOCK = 128
DIFF_HEADS = 4
DIFF_V_HEAD = 2 * HEAD_DIM
NSA_HEADS = 8
NSA_GROUPS = 2
NSA_HPG = NSA_HEADS // NSA_GROUPS
CMP_BLOCK = 32
CMP_STRIDE = 16
CMP_HIDDEN = 256
SEL_BLOCK = 64
SEL_TOPK = 8
WINDOW = 512
FORCED_SCORE = 1.0e4
DIFF_QK = DIFF_HEADS * 2 * HEAD_DIM
DIFF_VW = DIFF_HEADS * DIFF_V_HEAD
NSA_QW = NSA_HEADS * HEAD_DIM
NSA_KVW = NSA_GROUPS * HEAD_DIM
NSA_GATEW = 3 * NSA_HEADS
SPLIT_SIZES = (DIFF_QK, DIFF_QK, DIFF_VW, NSA_QW, NSA_KVW, NSA_KVW, NSA_KVW, NSA_KVW, NSA_KVW, NSA_KVW, NSA_GATEW, D_MODEL, D_MODEL)
IN_COLS = sum(SPLIT_SIZES)

kernel_name = "hybrid_diffattn_nsa_macaron"


def rms_norm(x, g=None):
    xf = x.astype(jnp.float32)
    y = xf * lax.rsqrt(jnp.mean(xf * xf, axis=-1, keepdims=True) + EPS)
    if g is not None:
        y = y * g.astype(jnp.float32)
    return y.astype(x.dtype)


def swiglu(x, w_gu, w_down):
    g, u = jnp.split(x @ w_gu, 2, axis=-1)
    return (jax.nn.silu(g) * u) @ w_down


def rope_partial(x, positions):
    half = ROPE_DIM // 2
    inv = ROPE_THETA ** (-2.0 * jnp.arange(half, dtype=jnp.float32) / ROPE_DIM)
    ang = positions.astype(jnp.float32)[..., None] * inv
    cos = jnp.cos(ang)[:, :, None, :]
    sin = jnp.sin(ang)[:, :, None, :]
    xr = x[..., :ROPE_DIM].astype(jnp.float32)
    x1, x2 = xr[..., :half], xr[..., half:]
    rot = jnp.concatenate([x1 * cos - x2 * sin, x2 * cos + x1 * sin], axis=-1).astype(x.dtype)
    return jnp.concatenate([rot, x[..., ROPE_DIM:]], axis=-1)


def masked_softmax(s, mask):
    s = jnp.where(mask, s.astype(jnp.float32), -jnp.inf)
    m = jnp.max(s, axis=-1, keepdims=True)
    m = jnp.where(jnp.isfinite(m), m, 0.0)
    p = jnp.exp(s - m)
    return p / jnp.maximum(jnp.sum(p, axis=-1, keepdims=True), 1e-30)


def diff_attention(q, k, v, lam, lam_init):
    B, T = q.shape[:2]
    nqb = T // Q_BLOCK
    scale = HEAD_DIM ** -0.5
    qb = q.reshape(B, nqb, Q_BLOCK, DIFF_HEADS, 2, HEAD_DIM).transpose(1, 0, 2, 3, 4, 5)
    kpos = jnp.arange(T)

    def block(args):
        qi, i = args
        qpos = i * Q_BLOCK + jnp.arange(Q_BLOCK)
        s = jnp.einsum('bqhmd,bkhmd->bhmqk', qi, k) * scale
        p = masked_softmax(s, kpos[None, :] <= qpos[:, None])
        a = p[:, :, 0] - lam * p[:, :, 1]
        return jnp.einsum('bhqk,bkhe->bqhe', a.astype(v.dtype), v)

    o = lax.map(block, (qb, jnp.arange(nqb)))
    o = o.transpose(1, 0, 2, 3, 4).reshape(B, T, DIFF_HEADS, DIFF_V_HEAD)
    o = rms_norm(o) * (1.0 - lam_init)
    return o.reshape(B, T, DIFF_VW)


def compress(x, pos_emb, w1, w2):
    B, T, G, dh = x.shape
    n_cmp = (T - CMP_BLOCK) // CMP_STRIDE + 1
    idx = jnp.arange(n_cmp)[:, None] * CMP_STRIDE + jnp.arange(CMP_BLOCK)[None, :]
    blocks = x[:, idx] + pos_emb[None, None, :, None, :]
    flat = blocks.transpose(0, 1, 3, 2, 4).reshape(B, n_cmp, G, CMP_BLOCK * dh)
    return jax.nn.silu(flat @ w1) @ w2


def nsa_attention(q, kc, vc, ks, vs, kw, vw, gates):
    B, T = q.shape[:2]
    G, Hg, dh = NSA_GROUPS, NSA_HPG, HEAD_DIM
    nqb = T // Q_BLOCK
    scale = dh ** -0.5
    n_cmp = kc.shape[1]
    n_sel = T // SEL_BLOCK
    topk = min(SEL_TOPK, n_sel)
    cmp_start = jnp.arange(n_cmp) * CMP_STRIDE
    cmp_end = cmp_start + CMP_BLOCK - 1
    sel_start = jnp.arange(n_sel) * SEL_BLOCK
    ov = jnp.clip(jnp.minimum(cmp_start[:, None] + CMP_BLOCK, sel_start[None, :] + SEL_BLOCK)
                  - jnp.maximum(cmp_start[:, None], sel_start[None, :]), 0)
    overlap = ov.astype(jnp.float32) / CMP_BLOCK
    ksb = ks.reshape(B, n_sel, SEL_BLOCK, G, dh).transpose(0, 3, 1, 2, 4)
    vsb = vs.reshape(B, n_sel, SEL_BLOCK, G, dh).transpose(0, 3, 1, 2, 4)
    pad = ((0, 0), (WINDOW, 0), (0, 0), (0, 0))
    kw_pad = jnp.pad(kw, pad)
    vw_pad = jnp.pad(vw, pad)
    qb = q.reshape(B, nqb, Q_BLOCK, G, Hg, dh).transpose(1, 0, 2, 3, 4, 5)
    gb = gates.reshape(B, nqb, Q_BLOCK, G, Hg, 3).transpose(1, 0, 2, 3, 4, 5)
    bidx = jnp.arange(B)[:, None, None, None]
    gidx = jnp.arange(G)[None, None, :, None]
    jsel = jnp.arange(n_sel)

    def block(args):
        qi, gi, i = args
        qpos = i * Q_BLOCK + jnp.arange(Q_BLOCK)
        s_c = jnp.einsum('bqghd,bcgd->bqghc', qi, kc) * scale
        m_c = (cmp_end[None, :] <= qpos[:, None])[None, :, None, None, :]
        p_c = masked_softmax(s_c, m_c)
        o_c = jnp.einsum('bqghc,bcgd->bqghd', p_c.astype(vc.dtype), vc)
        imp = jnp.einsum('bqghc,cj->bqgj', p_c, overlap)
        cur = qpos // SEL_BLOCK
        forced = (jsel[None, :] == 0) | (jsel[None, :] == cur[:, None]) | (jsel[None, :] == cur[:, None] - 1)
        future = sel_start[None, :] > qpos[:, None]
        imp = jnp.where(future[None, :, None, :], -1.0, jnp.where(forced[None, :, None, :], FORCED_SCORE, imp))
        _, sel = lax.top_k(imp, topk)
        kg = ksb[bidx, gidx, sel].reshape(B, Q_BLOCK, G, topk * SEL_BLOCK, dh)
        vg = vsb[bidx, gidx, sel].reshape(B, Q_BLOCK, G, topk * SEL_BLOCK, dh)
        tok = (sel[..., None] * SEL_BLOCK + jnp.arange(SEL_BLOCK)).reshape(B, Q_BLOCK, G, topk * SEL_BLOCK)
        m_s = (tok <= qpos[None, :, None, None])[:, :, :, None, :]
        s_s = jnp.einsum('bqghd,bqgkd->bqghk', qi, kg) * scale
        p_s = masked_softmax(s_s, m_s)
        o_s = jnp.einsum('bqghk,bqgkd->bqghd', p_s.astype(vg.dtype), vg)
        kwi = lax.dynamic_slice_in_dim(kw_pad, i * Q_BLOCK, WINDOW + Q_BLOCK, axis=1)
        vwi = lax.dynamic_slice_in_dim(vw_pad, i * Q_BLOCK, WINDOW + Q_BLOCK, axis=1)
        kpos = i * Q_BLOCK - WINDOW + jnp.arange(WINDOW + Q_BLOCK)
        m_w = ((kpos[None, :] <= qpos[:, None]) & (kpos[None, :] > qpos[:, None] - WINDOW)
               & (kpos[None, :] >= 0))[None, :, None, None, :]
        s_w = jnp.einsum('bqghd,bkgd->bqghk', qi, kwi) * scale
        p_w = masked_softmax(s_w, m_w)
        o_w = jnp.einsum('bqghk,bkgd->bqghd', p_w.astype(vwi.dtype), vwi)
        return gi[..., 0:1] * o_c + gi[..., 1:2] * o_s + gi[..., 2:3] * o_w

    o = lax.map(block, (qb, gb, jnp.arange(nqb)))
    return o.transpose(1, 0, 2, 3, 4, 5).reshape(B, T, NSA_QW)


def setup_inputs(seed: int = 0) -> dict:
    key = jax.random.key(seed)
    ks = jax.random.split(key, 24)
    n = lambda k, shape, s: jax.random.normal(k, shape, jnp.float32) * s
    gain = lambda k, shape: 1.0 + 0.02 * jax.random.normal(k, shape, jnp.float32)
    L = DEPTH
    return {
        "x": n(ks[0], (BATCH, SEQ, D_MODEL), 1.0),
        "positions": jnp.broadcast_to(jnp.arange(SEQ, dtype=jnp.int32), (BATCH, SEQ)),
        "ffn1_norm": gain(ks[1], (L, D_MODEL)),
        "ffn1_w_gu": n(ks[2], (L, D_MODEL, 2 * D_FF), D_MODEL ** -0.5),
        "ffn1_w_down": n(ks[3], (L, D_FF, D_MODEL), D_FF ** -0.5),
        "mix_norm": gain(ks[4], (L, D_MODEL)),
        "w_in": n(ks[5], (L, D_MODEL, IN_COLS), D_MODEL ** -0.5),
        "diff_lambda": n(ks[6], (L, 4, HEAD_DIM), 0.1),
        "cmp_pos": n(ks[7], (L, 2, CMP_BLOCK, HEAD_DIM), 0.02),
        "cmp_w1": n(ks[8], (L, 2, CMP_BLOCK * HEAD_DIM, CMP_HIDDEN), (CMP_BLOCK * HEAD_DIM) ** -0.5),
        "cmp_w2": n(ks[9], (L, 2, CMP_HIDDEN, HEAD_DIM), CMP_HIDDEN ** -0.5),
        "w_branch_a": n(ks[10], (L, DIFF_VW, D_MODEL), DIFF_VW ** -0.5),
        "w_branch_b": n(ks[11], (L, NSA_QW, D_MODEL), NSA_QW ** -0.5),
        "w_out": n(ks[12], (L, D_MODEL, D_MODEL), D_MODEL ** -0.5),
        "ffn2_norm": gain(ks[13], (L, D_MODEL)),
        "ffn2_w_gu": n(ks[14], (L, D_MODEL, 2 * D_FF), D_MODEL ** -0.5),
        "ffn2_w_down": n(ks[15], (L, D_FF, D_MODEL), D_FF ** -0.5),
        "final_norm": gain(ks[16], (D_MODEL,)),
    }


def reference(x, positions, ffn1_norm, ffn1_w_gu, ffn1_w_down, mix_norm, w_in, diff_lambda,
              cmp_pos, cmp_w1, cmp_w2, w_branch_a, w_branch_b, w_out,
              ffn2_norm, ffn2_w_gu, ffn2_w_down, final_norm):
    B, T, _ = x.shape
    split_points = np.cumsum(SPLIT_SIZES)[:-1].tolist()
    h = x
    for l in range(DEPTH):
        h = h + 0.5 * swiglu(rms_norm(h, ffn1_norm[l]), ffn1_w_gu[l], ffn1_w_down[l])
        u = rms_norm(h, mix_norm[l])
        (q_d, k_d, v_d, q_n, kc_raw, vc_raw, ks_, vs_, kw_, vw_, g_n, g_a, g_b) = jnp.split(u @ w_in[l], split_points, axis=-1)
        q_d = rope_partial(q_d.reshape(B, T, DIFF_HEADS * 2, HEAD_DIM), positions).reshape(B, T, DIFF_HEADS, 2, HEAD_DIM)
        k_d = rope_partial(k_d.reshape(B, T, DIFF_HEADS * 2, HEAD_DIM), positions).reshape(B, T, DIFF_HEADS, 2, HEAD_DIM)
        v_d = v_d.reshape(B, T, DIFF_HEADS, DIFF_V_HEAD)
        lam_init = 0.8 - 0.6 * math.exp(-0.3 * l)
        lp = diff_lambda[l].astype(jnp.float32)
        lam = jnp.exp(jnp.sum(lp[0] * lp[1])) - jnp.exp(jnp.sum(lp[2] * lp[3])) + lam_init
        o_a = diff_attention(q_d, k_d, v_d, lam, lam_init)
        kv_shape = (B, T, NSA_GROUPS, HEAD_DIM)
        q_n = rope_partial(q_n.reshape(B, T, NSA_HEADS, HEAD_DIM), positions)
        kc = compress(kc_raw.reshape(kv_shape), cmp_pos[l, 0], cmp_w1[l, 0], cmp_w2[l, 0])
        vc = compress(vc_raw.reshape(kv_shape), cmp_pos[l, 1], cmp_w1[l, 1], cmp_w2[l, 1])
        ks_r = rope_partial(ks_.reshape(kv_shape), positions)
        kw_r = rope_partial(kw_.reshape(kv_shape), positions)
        gates = jax.nn.sigmoid(g_n.reshape(B, T, NSA_HEADS, 3))
        o_b = nsa_attention(q_n, kc, vc, ks_r, vs_.reshape(kv_shape), kw_r, vw_.reshape(kv_shape), gates)
        y = jax.nn.sigmoid(g_a) * (o_a @ w_branch_a[l]) + jax.nn.sigmoid(g_b) * (o_b @ w_branch_b[l])
        h = h + y @ w_out[l]
        h = h + 0.5 * swiglu(rms_norm(h, ffn2_norm[l]), ffn2_w_gu[l], ffn2_w_down[l])
    return rms_norm(h, final_norm)
```

```python
import functools
import math

import numpy as np
import jax
import jax.numpy as jnp
from jax import lax
from jax.experimental import pallas as pl
from jax.experimental.pallas import tpu as pltpu

D_MODEL = 1024
D_FF = 2816
HEAD_DIM = 64
ROPE_DIM = HEAD_DIM // 4
ROPE_HALF = ROPE_DIM // 2
ROPE_THETA = 500000.0
EPS = 1e-6
DIFF_HEADS = 4
NSA_HEADS = 8
NSA_GROUPS = 2
NSA_HPG = NSA_HEADS // NSA_GROUPS
CMP_BLOCK = 32
CMP_STRIDE = 16
CMP_HIDDEN = 256
SEL_BLOCK = 64
SEL_SHIFT = SEL_BLOCK.bit_length() - 1
SEL_TOPK = 8
WINDOW = 512
FORCED_SCORE = 1.0e4
ATTN_SCALE = HEAD_DIM ** -0.5

_SPLITS = (512, 512, 512, 512, 128, 128, 128, 128, 128, 128, 24, 1024, 1024)
_OFFS = np.concatenate([[0], np.cumsum(_SPLITS)]).tolist()

LANES = 128
VMEM_LIMIT_BYTES = 56 * 1024 * 1024
TOKEN_TILE = 512
ATTN_Q_TILE = 128
ATTN_K_TILE = 128
SEL_PAD = 128

_C_QD, _C_KD, _C_QN, _C_KS, _C_KW = 0, 512, 1024, 1536, 1792
_C_ROPE_END = 2048
_C_VD, _C_VS, _C_VW = 2048, 2560, 2816
_C_CRAW, _C_GN, _C_GA, _C_GB = 3072, 3328, 3584, 4608
_C_TOTAL = 5632

_BF16 = jnp.bfloat16
_F32 = jnp.float32


def _params(n_grid):
    return pltpu.CompilerParams(
        dimension_semantics=("arbitrary",) * n_grid,
        vmem_limit_bytes=VMEM_LIMIT_BYTES,
    )


def _resident(shape, index_map):
    return pl.BlockSpec(shape, index_map, pipeline_mode=pl.Buffered(1))


def _dot(a, b):
    return jnp.dot(a, b, preferred_element_type=_F32)


def _dot_nt(a, b):
    return lax.dot_general(a, b, (((1,), (1,)), ((), ())), preferred_element_type=_F32)


def _sigmoid(x):
    return 1.0 / (1.0 + jnp.exp(-x))


def _rms_scale(x):
    return x * lax.rsqrt(jnp.mean(x * x, axis=-1, keepdims=True) + EPS)


def _rope_table_kernel(pos_ref, cos_ref, sin_ref):
    lane = lax.broadcasted_iota(jnp.int32, (1, LANES), 1)
    in_head = lane & (HEAD_DIM - 1)
    freq = (in_head & (ROPE_HALF - 1)).astype(_F32)
    inv = jnp.exp(freq * (-2.0 / ROPE_DIM * math.log(ROPE_THETA)))
    inv = jnp.where(in_head < ROPE_DIM, inv, 0.0)
    ang = pos_ref[...].astype(_F32) * inv
    cos_ref[...] = jnp.cos(ang)
    sin = jnp.sin(ang)
    sin_ref[...] = jnp.where(in_head < ROPE_HALF, -sin, sin)


def _rope_tables(positions):
    n = positions.size
    tm = min(TOKEN_TILE, n)
    pos = positions.reshape(n, 1)
    return pl.pallas_call(
        _rope_table_kernel,
        grid=(n // tm,),
        in_specs=[pl.BlockSpec((tm, 1), lambda i: (i, 0))],
        out_specs=[pl.BlockSpec((tm, LANES), lambda i: (i, 0))] * 2,
        out_shape=[jax.ShapeDtypeStruct((n, LANES), _F32)] * 2,
        compiler_params=_params(1),
        name="rope_tables",
    )(pos)


def _ffn_kernel(h_ref, g_ref, wg_ref, wu_ref, wd_ref, *rest, final):
    if final:
        fg_ref, o_ref = rest
    else:
        (o_ref,) = rest
    x = h_ref[...]
    u = (_rms_scale(x) * g_ref[...]).astype(_BF16)
    gate = _dot(u, wg_ref[...])
    up = _dot(u, wu_ref[...])
    act = (gate * _sigmoid(gate) * up).astype(_BF16)
    y = x + 0.5 * _dot(act, wd_ref[...])
    if final:
        y = _rms_scale(y) * fg_ref[...]
    o_ref[...] = y


def _ffn(h, g, wg, wu, wd, final_g=None):
    n = h.shape[0]
    tm = min(TOKEN_TILE, n)
    final = final_g is not None
    row = pl.BlockSpec((tm, D_MODEL), lambda i: (i, 0))
    vec = _resident((1, D_MODEL), lambda i: (0, 0))
    in_specs = [row, vec,
                _resident((D_MODEL, D_FF), lambda i: (0, 0)),
                _resident((D_MODEL, D_FF), lambda i: (0, 0)),
                _resident((D_FF, D_MODEL), lambda i: (0, 0))]
    args = [h, g.reshape(1, D_MODEL), wg, wu, wd]
    if final:
        in_specs.append(vec)
        args.append(final_g.reshape(1, D_MODEL))
    return pl.pallas_call(
        functools.partial(_ffn_kernel, final=final),
        grid=(n // tm,),
        in_specs=in_specs,
        out_specs=row,
        out_shape=jax.ShapeDtypeStruct((n, D_MODEL), _F32),
        compiler_params=_params(1),
        name="ffn_final" if final else "ffn",
    )(*args)


def _mix_in_kernel(h_ref, g_ref, cos_ref, sin_ref, w_ref,
                   qd_ref, kd_ref, qn_ref, ks_ref, kw_ref, vd_ref, vs_ref, vw_ref,
                   craw_ref, gn_ref, ga_ref, gb_ref):
    x = h_ref[...]
    u = (_rms_scale(x) * g_ref[...]).astype(_BF16)
    cos = cos_ref[...]
    sin = sin_ref[...]
    lane = lax.broadcasted_iota(jnp.int32, (1, LANES), 1)
    first_half = (lane & (HEAD_DIM - 1)) < ROPE_HALF

    def proj(c0, width):
        return _dot(u, w_ref[:, c0:c0 + width])

    def rope_store(o_ref, c0, width, scale):
        z = proj(c0, width)
        for c in range(width // LANES):
            zc = z[:, c * LANES:(c + 1) * LANES]
            partner = jnp.where(first_half, pltpu.roll(zc, LANES - ROPE_HALF, axis=1),
                                pltpu.roll(zc, ROPE_HALF, axis=1))
            r = zc * cos + partner * sin
            if scale != 1.0:
                r = r * scale
            o_ref[:, c * LANES:(c + 1) * LANES] = r.astype(o_ref.dtype)

    rope_store(qd_ref, _C_QD, 512, ATTN_SCALE)
    rope_store(kd_ref, _C_KD, 512, 1.0)
    rope_store(qn_ref, _C_QN, 512, ATTN_SCALE)
    rope_store(ks_ref, _C_KS, 256, 1.0)
    rope_store(kw_ref, _C_KW, 256, 1.0)
    vd_ref[...] = proj(_C_VD, 512).astype(_BF16)
    vs_ref[...] = proj(_C_VS, 256).astype(_BF16)
    vw_ref[...] = proj(_C_VW, 256).astype(_BF16)
    craw = proj(_C_CRAW, 256)
    craw_ref[0] = craw[:, :LANES]
    craw_ref[1] = craw[:, LANES:]
    gn_ref[...] = _sigmoid(proj(_C_GN, 256))
    ga_ref[...] = _sigmoid(proj(_C_GA, D_MODEL)).astype(_BF16)
    gb_ref[...] = _sigmoid(proj(_C_GB, D_MODEL)).astype(_BF16)


def _mix_in(h, g, cos_t, sin_t, w):
    n = h.shape[0]
    tm = min(TOKEN_TILE, n)

    def row(width):
        return pl.BlockSpec((tm, width), lambda i: (i, 0))

    out_widths = [(512, _BF16), (512, _BF16), (512, _BF16), (256, _BF16), (256, _BF16),
                  (512, _BF16), (256, _BF16), (256, _BF16)]
    out_specs = [row(wd) for wd, _ in out_widths]
    out_shape = [jax.ShapeDtypeStruct((n, wd), dt) for wd, dt in out_widths]
    out_specs += [pl.BlockSpec((2, tm, LANES), lambda i: (0, i, 0)), row(256), row(D_MODEL), row(D_MODEL)]
    out_shape += [jax.ShapeDtypeStruct((2, n, LANES), _F32),
                  jax.ShapeDtypeStruct((n, 256), _F32),
                  jax.ShapeDtypeStruct((n, D_MODEL), _BF16),
                  jax.ShapeDtypeStruct((n, D_MODEL), _BF16)]
    return pl.pallas_call(
        _mix_in_kernel,
        grid=(n // tm,),
        in_specs=[row(D_MODEL), _resident((1, D_MODEL), lambda i: (0, 0)),
                  row(LANES), row(LANES),
                  _resident((D_MODEL, _C_TOTAL), lambda i: (0, 0))],
        out_specs=out_specs,
        out_shape=out_shape,
        compiler_params=_params(1),
        name="mix_in",
    )(h, g.reshape(1, D_MODEL), cos_t, sin_t, w)


def _compress_kernel(x_ref, pos_ref, w1_ref, w2_ref, o_ref, *, n_cmp):
    x = x_ref[0, 0]
    rows = x.shape[0]
    x_top = (x + pos_ref[0, 0]).astype(_BF16)
    x_bot = (x + pos_ref[0, 1]).astype(_BF16)
    valid = lax.broadcasted_iota(jnp.int32, (rows, 1), 0) < n_cmp
    for g in range(NSA_GROUPS):
        top = _dot(x_top, w1_ref[0, g, 0])
        bot = _dot(x_bot, w1_ref[0, g, 1])
        pre = top + pltpu.roll(bot, rows - 1, axis=0)
        hid = (pre * _sigmoid(pre)).astype(_BF16)
        out = _dot(hid, w2_ref[0])
        out = jnp.where(valid, out, 0.0)
        o_ref[0, 0, :, g * LANES:(g + 1) * LANES] = out.astype(o_ref.dtype)


def _compress(craw, pos_x, w1_x, w2_x, batch, seq):
    rows = seq // CMP_STRIDE
    n_cmp = (seq - CMP_BLOCK) // CMP_STRIDE + 1
    width = CMP_STRIDE * LANES
    x = craw.reshape(2, batch, rows, width)
    return pl.pallas_call(
        functools.partial(_compress_kernel, n_cmp=n_cmp),
        grid=(2, batch),
        in_specs=[pl.BlockSpec((1, 1, rows, width), lambda k, b: (k, b, 0, 0)),
                  pl.BlockSpec((1, 2, 1, width), lambda k, b: (k, 0, 0, 0)),
                  pl.BlockSpec((1, NSA_GROUPS, 2, width, CMP_HIDDEN), lambda k, b: (k, 0, 0, 0, 0)),
                  pl.BlockSpec((1, CMP_HIDDEN, LANES), lambda k, b: (k, 0, 0))],
        out_specs=pl.BlockSpec((1, 1, rows, NSA_GROUPS * LANES), lambda k, b: (k, b, 0, 0)),
        out_shape=jax.ShapeDtypeStruct((2, batch, rows, NSA_GROUPS * LANES), _BF16),
        compiler_params=_params(2),
        name="compress",
    )(x, pos_x, w1_x, w2_x)


def _mask_bias(mask):
    return jnp.where(mask, 0.0, -jnp.inf).astype(_F32)


def _softmax_step(carry, s, bias, v):
    m, l, acc = carry
    if bias is not None:
        s = s + bias
    m_new = jnp.maximum(m, jnp.max(s, axis=1, keepdims=True))
    m_safe = jnp.where(m_new == -jnp.inf, 0.0, m_new)
    alpha = jnp.exp(m - m_safe)
    p = jnp.exp(s - m_safe)
    l = alpha * l + jnp.sum(p, axis=1, keepdims=True)
    acc = alpha * acc + _dot(p.astype(_BF16), v)
    return m_new, l, acc


def _softmax_init(rows, width):
    return (jnp.full((rows, 1), -jnp.inf, _F32), jnp.zeros((rows, 1), _F32),
            jnp.zeros((rows, width), _F32))


def _softmax_finish(carry):
    _, l, acc = carry
    return acc / jnp.maximum(l, 1e-30)


def _diff_attn_kernel(lam_ref, q_ref, k_ref, v_ref, o_ref, *, lam_init):
    i = pl.program_id(2)
    tq, tk = ATTN_Q_TILE, ATTN_K_TILE
    lane = lax.broadcasted_iota(jnp.int32, (1, LANES), 1)
    q = q_ref[...]
    zero = jnp.zeros_like(q)
    qs = jnp.concatenate([jnp.where(lane < HEAD_DIM, q, zero), jnp.where(lane >= HEAD_DIM, q, zero)], axis=0)

    def chunk(j, carry, diagonal):
        k0 = pl.multiple_of(j * tk, tk)
        k = k_ref[pl.ds(k0, tk), :]
        v = v_ref[pl.ds(k0, tk), :]
        s = _dot_nt(qs, k)
        bias = None
        if diagonal:
            r = lax.broadcasted_iota(jnp.int32, (tq, tk), 0)
            c = lax.broadcasted_iota(jnp.int32, (tq, tk), 1)
            tri = _mask_bias(c <= r)
            bias = jnp.concatenate([tri, tri], axis=0)
        return _softmax_step(carry, s, bias, v)

    carry = lax.fori_loop(0, i, lambda j, c: chunk(j, c, False), _softmax_init(2 * tq, LANES))
    carry = chunk(i, carry, True)
    o = _softmax_finish(carry)
    lp = lam_ref[...]
    lam = (jnp.exp(jnp.sum(lp[0:1] * lp[1:2], axis=1, keepdims=True))
           - jnp.exp(jnp.sum(lp[2:3] * lp[3:4], axis=1, keepdims=True)) + lam_init)
    o = o[:tq] - lam * o[tq:]
    o_ref[...] = (_rms_scale(o) * (1.0 - lam_init)).astype(o_ref.dtype)


def _diff_attn(lam_p, qd, kd, vd, batch, seq, lam_init):
    tq = ATTN_Q_TILE
    nq = seq // tq
    n = batch * seq
    return pl.pallas_call(
        functools.partial(_diff_attn_kernel, lam_init=lam_init),
        grid=(batch, DIFF_HEADS, nq),
        in_specs=[pl.BlockSpec((4, HEAD_DIM), lambda b, h, i: (0, 0)),
                  pl.BlockSpec((tq, LANES), lambda b, h, i: (b * nq + i, h)),
                  pl.BlockSpec((seq, LANES), lambda b, h, i: (b, h)),
                  pl.BlockSpec((seq, LANES), lambda b, h, i: (b, h))],
        out_specs=pl.BlockSpec((tq, LANES), lambda b, h, i: (b * nq + i, h)),
        out_shape=jax.ShapeDtypeStruct((n, DIFF_HEADS * LANES), _BF16),
        compiler_params=_params(3),
        name="diff_attn",
    )(lam_p, qd, kd, vd)


def _nsa_kernel(q_ref, gate_ref, ov_ref, kc_ref, vc_ref, ks_ref, vs_ref, kw_ref, vw_ref, o_ref):
    i = pl.program_id(2)
    tq, tk = ATTN_Q_TILE, ATTN_K_TILE
    hpg = NSA_HPG
    lane = lax.broadcasted_iota(jnp.int32, (1, LANES), 1)
    low = lane < HEAD_DIM
    q = q_ref[...]
    parts = []
    for j in range(hpg):
        qc = q[:, (j // 2) * LANES:(j // 2 + 1) * LANES]
        keep = low if j % 2 == 0 else jnp.logical_not(low)
        parts.append(jnp.where(keep, qc, jnp.zeros_like(qc)))
    qs = jnp.concatenate(parts, axis=0)
    rows = hpg * tq
    q0 = i * tq
    qpos1 = q0 + lax.broadcasted_iota(jnp.int32, (tq, 1), 0)

    def per_head(x):
        return jnp.concatenate([x] * hpg, axis=0)

    kc = kc_ref[0, 0]
    vc = vc_ref[0, 0]
    ncp = kc.shape[0]
    s_c = _dot_nt(qs, kc)
    cmp_end = lax.broadcasted_iota(jnp.int32, (1, ncp), 1) * CMP_STRIDE + (CMP_BLOCK - 1)
    s_c = s_c + per_head(_mask_bias(cmp_end <= qpos1))
    mx = jnp.max(s_c, axis=1, keepdims=True)
    mx = jnp.where(mx == -jnp.inf, 0.0, mx)
    p_c = jnp.exp(s_c - mx)
    p_c = p_c / jnp.maximum(jnp.sum(p_c, axis=1, keepdims=True), 1e-30)
    o_c = _dot(p_c.astype(_BF16), vc)

    p_sum = p_c[0:tq]
    for j in range(1, hpg):
        p_sum = p_sum + p_c[j * tq:(j + 1) * tq]
    ov = ov_ref[...]
    p_hi = p_sum.astype(_BF16)
    r1 = p_sum - p_hi.astype(_F32)
    p_mid = r1.astype(_BF16)
    p_lo = (r1 - p_mid.astype(_F32)).astype(_BF16)
    imp = _dot(p_hi, ov) + _dot(p_mid, ov) + _dot(p_lo, ov)

    jsel = lax.broadcasted_iota(jnp.int32, (1, SEL_PAD), 1)
    cur = qpos1 >> SEL_SHIFT
    forced = (jsel == 0) | (jsel == cur) | (jsel == cur - 1)
    future = jsel * SEL_BLOCK > qpos1
    work = jnp.where(future, -1.0, jnp.where(forced, FORCED_SCORE, imp))

    jsel_f = jsel.astype(_F32)
    chosen = jnp.zeros((tq, SEL_PAD), _F32)
    for _ in range(SEL_TOPK):
        best = jnp.max(work, axis=1, keepdims=True)
        first = jnp.min(jnp.where(work == best, jsel_f, float(SEL_PAD)), axis=1, keepdims=True)
        pick = jsel_f == first
        chosen = jnp.where(pick, 1.0, chosen)
        work = jnp.where(pick, -jnp.inf, work)
    chosen = chosen.astype(_BF16)

    r_t = lax.broadcasted_iota(jnp.int32, (tq, tk), 0)
    c_t = lax.broadcasted_iota(jnp.int32, (tq, tk), 1)
    blk_row = lax.broadcasted_iota(jnp.int32, (SEL_PAD, tk), 0)
    blk_col = lax.broadcasted_iota(jnp.int32, (SEL_PAD, tk), 1)
    causal_bias = _mask_bias(c_t <= r_t)

    def sel_chunk(j, carry, diagonal):
        k0 = pl.multiple_of(j * tk, tk)
        k = ks_ref[pl.ds(k0, tk), :]
        v = vs_ref[pl.ds(k0, tk), :]
        expand = jnp.where(blk_row == ((k0 + blk_col) >> SEL_SHIFT), 1.0, 0.0).astype(_BF16)
        bias = _mask_bias(_dot(chosen, expand) > 0.5)
        if diagonal:
            bias = bias + causal_bias
        s = _dot_nt(qs, k)
        return _softmax_step(carry, s, per_head(bias), v)

    carry = lax.fori_loop(0, i, lambda j, c: sel_chunk(j, c, False), _softmax_init(rows, LANES))
    o_s = _softmax_finish(sel_chunk(i, carry, True))

    n_back = WINDOW // tk

    def win_chunk(j, carry, mask):
        k0 = pl.multiple_of(j * tk, tk)
        s = _dot_nt(qs, kw_ref[pl.ds(k0, tk), :])
        return _softmax_step(carry, s, mask, vw_ref[pl.ds(k0, tk), :])

    carry = _softmax_init(rows, LANES)
    oldest = i - n_back
    carry = lax.cond(oldest >= 0,
                     lambda c: win_chunk(oldest, c, per_head(_mask_bias(c_t > r_t))),
                     lambda c: c, carry)
    carry = lax.fori_loop(jnp.maximum(oldest + 1, 0), i, lambda j, c: win_chunk(j, c, None), carry)
    o_w = _softmax_finish(win_chunk(i, carry, per_head(causal_bias)))

    gate = gate_ref[...]
    heads = []
    for j in range(hpg):
        sl = slice(j * tq, (j + 1) * tq)
        heads.append(gate[:, 3 * j:3 * j + 1] * o_c[sl] + gate[:, 3 * j + 1:3 * j + 2] * o_s[sl]
                     + gate[:, 3 * j + 2:3 * j + 3] * o_w[sl])
    for c in range(hpg // 2):
        o_ref[:, c * LANES:(c + 1) * LANES] = jnp.where(low, heads[2 * c], heads[2 * c + 1]).astype(o_ref.dtype)


def _overlap_matrix(seq):
    rows = seq // CMP_STRIDE
    n_cmp = (seq - CMP_BLOCK) // CMP_STRIDE + 1
    n_sel = seq // SEL_BLOCK
    cs = np.arange(rows)[:, None] * CMP_STRIDE
    ss = np.arange(SEL_PAD)[None, :] * SEL_BLOCK
    ov = np.clip(np.minimum(cs + CMP_BLOCK, ss + SEL_BLOCK) - np.maximum(cs, ss), 0, None) / CMP_BLOCK
    ov = ov * (np.arange(rows)[:, None] < n_cmp) * (np.arange(SEL_PAD)[None, :] < n_sel)
    return jnp.asarray(ov, dtype=_BF16)


def _nsa(qn, gn, ckv, ks, vs, kw, vw, batch, seq):
    tq = ATTN_Q_TILE
    nq = seq // tq
    n = batch * seq
    rows = seq // CMP_STRIDE
    assert seq // SEL_BLOCK <= SEL_PAD
    qmap = lambda b, g, i: (b * nq + i, g)
    kvmap = lambda b, g, i: (b, g)
    return pl.pallas_call(
        _nsa_kernel,
        grid=(batch, NSA_GROUPS, nq),
        in_specs=[pl.BlockSpec((tq, NSA_HPG * HEAD_DIM), qmap),
                  pl.BlockSpec((tq, LANES), qmap),
                  pl.BlockSpec((rows, SEL_PAD), lambda b, g, i: (0, 0)),
                  pl.BlockSpec((1, 1, rows, LANES), lambda b, g, i: (0, b, 0, g)),
                  pl.BlockSpec((1, 1, rows, LANES), lambda b, g, i: (1, b, 0, g)),
                  pl.BlockSpec((seq, LANES), kvmap), pl.BlockSpec((seq, LANES), kvmap),
                  pl.BlockSpec((seq, LANES), kvmap), pl.BlockSpec((seq, LANES), kvmap)],
        out_specs=pl.BlockSpec((tq, NSA_HPG * HEAD_DIM), qmap),
        out_shape=jax.ShapeDtypeStruct((n, NSA_HEADS * HEAD_DIM), _BF16),
        compiler_params=_params(3),
        name="nsa",
    )(qn, gn, _overlap_matrix(seq), ckv, ckv, ks, vs, kw, vw)


def _merge_kernel(h_ref, oa_ref, ob_ref, ga_ref, gb_ref, wa_ref, wb_ref, wo_ref, o_ref):
    y = (ga_ref[...].astype(_F32) * _dot(oa_ref[...], wa_ref[...])
         + gb_ref[...].astype(_F32) * _dot(ob_ref[...], wb_ref[...]))
    o_ref[...] = h_ref[...] + _dot(y.astype(_BF16), wo_ref[...])


def _merge(h, oa, ob, ga, gb, wa, wb, wo):
    n = h.shape[0]
    tm = min(TOKEN_TILE, n)

    def row(width):
        return pl.BlockSpec((tm, width), lambda i: (i, 0))

    return pl.pallas_call(
        _merge_kernel,
        grid=(n // tm,),
        in_specs=[row(D_MODEL), row(512), row(512), row(D_MODEL), row(D_MODEL),
                  _resident((512, D_MODEL), lambda i: (0, 0)),
                  _resident((512, D_MODEL), lambda i: (0, 0)),
                  _resident((D_MODEL, D_MODEL), lambda i: (0, 0))],
        out_specs=row(D_MODEL),
        out_shape=jax.ShapeDtypeStruct((n, D_MODEL), _F32),
        compiler_params=_params(1),
        name="merge",
    )(h, oa, ob, ga, gb, wa, wb, wo)


def _dup_groups(w):
    a, b = w[:, :HEAD_DIM], w[:, HEAD_DIM:]
    return jnp.concatenate([a, a, b, b], axis=1)


def _prep_w_in(w):
    col = lambda k: w[:, _OFFS[k]:_OFFS[k + 1]]
    q_d, k_d, v_d, q_n, kc, vc, ks, vs, kw, vw, g_n, g_a, g_b = [col(k) for k in range(13)]
    per_group = NSA_HPG * 3
    pad = jnp.zeros((w.shape[0], LANES - per_group), w.dtype)
    gn = jnp.concatenate([g_n[:, :per_group], pad, g_n[:, per_group:], pad], axis=1)
    out = jnp.concatenate([q_d, k_d, q_n, _dup_groups(ks), _dup_groups(kw),
                           v_d, _dup_groups(vs), _dup_groups(vw), kc, vc, gn, g_a, g_b], axis=1)
    assert out.shape[1] == _C_TOTAL
    return out.astype(_BF16)


def _prep_compress(pos, w1, w2):
    half = CMP_BLOCK // 2
    w1 = w1.reshape(2, 2, half, HEAD_DIM, CMP_HIDDEN)
    zeros = jnp.zeros_like(w1)
    per_group = []
    for g in range(NSA_GROUPS):
        lanes = [zeros, zeros]
        lanes[g] = w1
        per_group.append(jnp.concatenate(lanes, axis=3))
    w1_x = jnp.stack(per_group, axis=1).reshape(2, NSA_GROUPS, 2, half * LANES, CMP_HIDDEN).astype(_BF16)
    pos = pos.reshape(2, 2, half, HEAD_DIM)
    pos_x = jnp.concatenate([pos, pos], axis=3).reshape(2, 2, 1, half * LANES).astype(_F32)
    w2_x = jnp.concatenate([w2, w2], axis=2).astype(_BF16)
    return pos_x, w1_x, w2_x


def kernel(x, positions, ffn1_norm, ffn1_w_gu, ffn1_w_down, mix_norm, w_in, diff_lambda, cmp_pos, cmp_w1, cmp_w2, w_branch_a, w_branch_b, w_out, ffn2_norm, ffn2_w_gu, ffn2_w_down, final_norm):
    batch, seq, _ = x.shape
    n = batch * seq
    depth = w_in.shape[0]
    assert seq % max(ATTN_Q_TILE, CMP_STRIDE * 8) == 0 and WINDOW % ATTN_K_TILE == 0
    assert ATTN_Q_TILE == ATTN_K_TILE
    cos_t, sin_t = _rope_tables(positions)
    h = x.reshape(n, D_MODEL)
    for l in range(depth):
        bf = lambda a: a.astype(_BF16)
        h = _ffn(h, ffn1_norm[l], bf(ffn1_w_gu[l][:, :D_FF]), bf(ffn1_w_gu[l][:, D_FF:]), bf(ffn1_w_down[l]))
        (qd, kd, qn, ks, kw, vd, vs, vw, craw, gn, ga, gb) = _mix_in(h, mix_norm[l], cos_t, sin_t, _prep_w_in(w_in[l]))
        ckv = _compress(craw, *_prep_compress(cmp_pos[l], cmp_w1[l], cmp_w2[l]), batch, seq)
        lam_init = 0.8 - 0.6 * math.exp(-0.3 * l)
        o_a = _diff_attn(diff_lambda[l].astype(_F32), qd, kd, vd, batch, seq, lam_init)
        o_b = _nsa(qn, gn, ckv, ks, vs, kw, vw, batch, seq)
        h = _merge(h, o_a, o_b, ga, gb, bf(w_branch_a[l]), bf(w_branch_b[l]), bf(w_out[l]))
        last = l == depth - 1
        h = _ffn(h, ffn2_norm[l], bf(ffn2_w_gu[l][:, :D_FF]), bf(ffn2_w_gu[l][:, D_FF:]), bf(ffn2_w_down[l]),
                 final_g=final_norm if last else None)
    return h.reshape(batch, seq, D_MODEL)
```

```python
import functools
import math

import numpy as np
import jax
import jax.numpy as jnp
from jax import lax
from jax.experimental import pallas as pl
from jax.experimental.pallas import tpu as pltpu

D_MODEL = 1024
D_FF = 2816
HEAD_DIM = 64
ROPE_DIM = HEAD_DIM // 4
ROPE_HALF = ROPE_DIM // 2
ROPE_THETA = 500000.0
EPS = 1e-6
DIFF_HEADS = 4
NSA_HEADS = 8
NSA_GROUPS = 2
NSA_HPG = NSA_HEADS // NSA_GROUPS
CMP_BLOCK = 32
CMP_STRIDE = 16
CMP_HIDDEN = 256
SEL_BLOCK = 64
SEL_SHIFT = SEL_BLOCK.bit_length() - 1
SEL_TOPK = 8
WINDOW = 512
FORCED_SCORE = 1.0e4
MASKED_SCORE = 2.0 ** 100
ATTN_SCALE = HEAD_DIM ** -0.5

_SPLITS = (512, 512, 512, 512, 128, 128, 128, 128, 128, 128, 24, 1024, 1024)
_OFFS = np.concatenate([[0], np.cumsum(_SPLITS)]).tolist()

LANES = 128
VMEM_LIMIT_BYTES = 56 * 1024 * 1024
TOKEN_TILE = 512
DIFF_TILE = 256
NSA_TILE = 256
SEL_PAD = 128

_C_QD, _C_KD, _C_QN, _C_KS, _C_KW = 0, 512, 1024, 1536, 1792
_C_ROPE_END = 2048
_C_VD, _C_VS, _C_VW = 2048, 2560, 2816
_C_CRAW, _C_GN, _C_GA, _C_GB = 3072, 3328, 3584, 4608
_C_TOTAL = 5632

_BF16 = jnp.bfloat16
_F32 = jnp.float32


def _params(n_grid):
    return pltpu.CompilerParams(
        dimension_semantics=("arbitrary",) * n_grid,
        vmem_limit_bytes=VMEM_LIMIT_BYTES,
    )


def _resident(shape, index_map):
    return pl.BlockSpec(shape, index_map, pipeline_mode=pl.Buffered(1))


def _dot(a, b):
    return jnp.dot(a, b, preferred_element_type=_F32)


def _dot_nt(a, b):
    return lax.dot_general(a, b, (((1,), (1,)), ((), ())), preferred_element_type=_F32)


def _sigmoid(x):
    return 1.0 / (1.0 + jnp.exp(-x))


def _rms_scale(x):
    return x * lax.rsqrt(jnp.mean(x * x, axis=-1, keepdims=True) + EPS)


def _rope_table_kernel(pos_ref, cos_ref, sin_ref):
    lane = lax.broadcasted_iota(jnp.int32, (1, LANES), 1)
    in_head = lane & (HEAD_DIM - 1)
    freq = (in_head & (ROPE_HALF - 1)).astype(_F32)
    inv = jnp.exp(freq * (-2.0 / ROPE_DIM * math.log(ROPE_THETA)))
    inv = jnp.where(in_head < ROPE_DIM, inv, 0.0)
    ang = pos_ref[...].astype(_F32) * inv
    cos_ref[...] = jnp.cos(ang)
    sin = jnp.sin(ang)
    sin_ref[...] = jnp.where(in_head < ROPE_HALF, -sin, sin)


def _rope_tables(positions):
    n = positions.size
    tm = min(TOKEN_TILE, n)
    pos = positions.reshape(n, 1)
    return pl.pallas_call(
        _rope_table_kernel,
        grid=(n // tm,),
        in_specs=[pl.BlockSpec((tm, 1), lambda i: (i, 0))],
        out_specs=[pl.BlockSpec((tm, LANES), lambda i: (i, 0))] * 2,
        out_shape=[jax.ShapeDtypeStruct((n, LANES), _F32)] * 2,
        compiler_params=_params(1),
        name="rope_tables",
    )(pos)


def _ffn_kernel(h_ref, g_ref, wg_ref, wu_ref, wd_ref, *rest, final):
    if final:
        fg_ref, o_ref = rest
    else:
        (o_ref,) = rest
    x = h_ref[...]
    u = (_rms_scale(x) * g_ref[...]).astype(_BF16)
    gate = _dot(u, wg_ref[...])
    up = _dot(u, wu_ref[...])
    act = (gate * _sigmoid(gate) * up).astype(_BF16)
    y = x + 0.5 * _dot(act, wd_ref[...])
    if final:
        y = _rms_scale(y) * fg_ref[...]
    o_ref[...] = y


def _ffn(h, g, wg, wu, wd, final_g=None):
    n = h.shape[0]
    tm = min(TOKEN_TILE, n)
    final = final_g is not None
    row = pl.BlockSpec((tm, D_MODEL), lambda i: (i, 0))
    vec = _resident((1, D_MODEL), lambda i: (0, 0))
    in_specs = [row, vec,
                _resident((D_MODEL, D_FF), lambda i: (0, 0)),
                _resident((D_MODEL, D_FF), lambda i: (0, 0)),
                _resident((D_FF, D_MODEL), lambda i: (0, 0))]
    args = [h, g.reshape(1, D_MODEL), wg, wu, wd]
    if final:
        in_specs.append(vec)
        args.append(final_g.reshape(1, D_MODEL))
    return pl.pallas_call(
        functools.partial(_ffn_kernel, final=final),
        grid=(n // tm,),
        in_specs=in_specs,
        out_specs=row,
        out_shape=jax.ShapeDtypeStruct((n, D_MODEL), _F32),
        compiler_params=_params(1),
        name="ffn_final" if final else "ffn",
    )(*args)


def _mix_in_kernel(h_ref, g_ref, cos_ref, sin_ref, w_ref,
                   qd_ref, kd_ref, qn_ref, ks_ref, kw_ref, vd_ref, vs_ref, vw_ref,
                   craw_ref, gn_ref, ga_ref, gb_ref):
    x = h_ref[...]
    u = (_rms_scale(x) * g_ref[...]).astype(_BF16)
    cos = cos_ref[...]
    sin = sin_ref[...]
    lane = lax.broadcasted_iota(jnp.int32, (1, LANES), 1)
    first_half = (lane & (HEAD_DIM - 1)) < ROPE_HALF

    def proj(c0, width):
        return _dot(u, w_ref[:, c0:c0 + width])

    def rope_store(o_ref, c0, width, scale):
        z = proj(c0, width)
        for c in range(width // LANES):
            zc = z[:, c * LANES:(c + 1) * LANES]
            partner = jnp.where(first_half, pltpu.roll(zc, LANES - ROPE_HALF, axis=1),
                                pltpu.roll(zc, ROPE_HALF, axis=1))
            r = zc * cos + partner * sin
            if scale != 1.0:
                r = r * scale
            o_ref[:, c * LANES:(c + 1) * LANES] = r.astype(o_ref.dtype)

    rope_store(qd_ref, _C_QD, 512, ATTN_SCALE)
    rope_store(kd_ref, _C_KD, 512, 1.0)
    rope_store(qn_ref, _C_QN, 512, ATTN_SCALE)
    rope_store(ks_ref, _C_KS, 256, 1.0)
    rope_store(kw_ref, _C_KW, 256, 1.0)
    vd_ref[...] = proj(_C_VD, 512).astype(_BF16)
    vs_ref[...] = proj(_C_VS, 256).astype(_BF16)
    vw_ref[...] = proj(_C_VW, 256).astype(_BF16)
    craw = proj(_C_CRAW, 256)
    craw_ref[0] = craw[:, :LANES]
    craw_ref[1] = craw[:, LANES:]
    gn_ref[...] = _sigmoid(proj(_C_GN, 256))
    ga_ref[...] = _sigmoid(proj(_C_GA, D_MODEL)).astype(_BF16)
    gb_ref[...] = _sigmoid(proj(_C_GB, D_MODEL)).astype(_BF16)


def _mix_in(h, g, cos_t, sin_t, w):
    n = h.shape[0]
    tm = min(TOKEN_TILE, n)

    def row(width):
        return pl.BlockSpec((tm, width), lambda i: (i, 0))

    out_widths = [(512, _BF16), (512, _BF16), (512, _BF16), (256, _BF16), (256, _BF16),
                  (512, _BF16), (256, _BF16), (256, _BF16)]
    out_specs = [row(wd) for wd, _ in out_widths]
    out_shape = [jax.ShapeDtypeStruct((n, wd), dt) for wd, dt in out_widths]
    out_specs += [pl.BlockSpec((2, tm, LANES), lambda i: (0, i, 0)), row(256), row(D_MODEL), row(D_MODEL)]
    out_shape += [jax.ShapeDtypeStruct((2, n, LANES), _F32),
                  jax.ShapeDtypeStruct((n, 256), _F32),
                  jax.ShapeDtypeStruct((n, D_MODEL), _BF16),
                  jax.ShapeDtypeStruct((n, D_MODEL), _BF16)]
    return pl.pallas_call(
        _mix_in_kernel,
        grid=(n // tm,),
        in_specs=[row(D_MODEL), _resident((1, D_MODEL), lambda i: (0, 0)),
                  row(LANES), row(LANES),
                  _resident((D_MODEL, _C_TOTAL), lambda i: (0, 0))],
        out_specs=out_specs,
        out_shape=out_shape,
        compiler_params=_params(1),
        name="mix_in",
    )(h, g.reshape(1, D_MODEL), cos_t, sin_t, w)


def _compress_kernel(x_ref, pos_ref, w1_ref, w2_ref, o_ref, *, n_cmp):
    x = x_ref[0, 0]
    rows = x.shape[0]
    x_top = (x + pos_ref[0, 0]).astype(_BF16)
    x_bot = (x + pos_ref[0, 1]).astype(_BF16)
    valid = lax.broadcasted_iota(jnp.int32, (rows, 1), 0) < n_cmp
    for g in range(NSA_GROUPS):
        top = _dot(x_top, w1_ref[0, g, 0])
        bot = _dot(x_bot, w1_ref[0, g, 1])
        pre = top + pltpu.roll(bot, rows - 1, axis=0)
        hid = (pre * _sigmoid(pre)).astype(_BF16)
        out = _dot(hid, w2_ref[0])
        out = jnp.where(valid, out, 0.0)
        o_ref[0, 0, :, g * LANES:(g + 1) * LANES] = out.astype(o_ref.dtype)


def _compress(craw, pos_x, w1_x, w2_x, batch, seq):
    rows = seq // CMP_STRIDE
    n_cmp = (seq - CMP_BLOCK) // CMP_STRIDE + 1
    width = CMP_STRIDE * LANES
    x = craw.reshape(2, batch, rows, width)
    return pl.pallas_call(
        functools.partial(_compress_kernel, n_cmp=n_cmp),
        grid=(2, batch),
        in_specs=[pl.BlockSpec((1, 1, rows, width), lambda k, b: (k, b, 0, 0)),
                  pl.BlockSpec((1, 2, 1, width), lambda k, b: (k, 0, 0, 0)),
                  pl.BlockSpec((1, NSA_GROUPS, 2, width, CMP_HIDDEN), lambda k, b: (k, 0, 0, 0, 0)),
                  pl.BlockSpec((1, CMP_HIDDEN, LANES), lambda k, b: (k, 0, 0))],
        out_specs=pl.BlockSpec((1, 1, rows, NSA_GROUPS * LANES), lambda k, b: (k, b, 0, 0)),
        out_shape=jax.ShapeDtypeStruct((2, batch, rows, NSA_GROUPS * LANES), _BF16),
        compiler_params=_params(2),
        name="compress",
    )(x, pos_x, w1_x, w2_x)


def _mask_bias(mask):
    return jnp.where(mask, 0.0, -jnp.inf).astype(_F32)


def _diff_attn_kernel(lam_ref, q_ref, k_ref, v_ref, o_ref, *, lam_init):
    i = pl.program_id(1)
    tq = DIFF_TILE
    lane = lax.broadcasted_iota(jnp.int32, (1, LANES), 1)
    ones = jnp.ones((tq, LANES), _BF16)
    heads = range(DIFF_HEADS)

    def head_cols(ref, rows, h):
        return ref[rows, h * LANES:(h + 1) * LANES]

    qs = []
    for h in heads:
        q = head_cols(q_ref, slice(None), h)
        zero = jnp.zeros_like(q)
        qs.append(jnp.concatenate([jnp.where(lane < HEAD_DIM, q, zero), jnp.where(lane >= HEAD_DIM, q, zero)], axis=0))

    def scores(h, rows):
        return _dot_nt(qs[h], head_cols(k_ref, rows, h))

    def weighted(h, p, rows):
        v_one = jnp.concatenate([head_cols(v_ref, rows, h), ones], axis=1)
        return _dot(p.astype(_BF16), v_one)

    r = lax.broadcasted_iota(jnp.int32, (tq, tq), 0)
    c = lax.broadcasted_iota(jnp.int32, (tq, tq), 1)
    tri = _mask_bias(c <= r)
    tri2 = jnp.concatenate([tri, tri], axis=0)
    diag = pl.ds(pl.multiple_of(i * tq, tq), tq)
    carry = []
    for h in heads:
        s = scores(h, diag) + tri2
        m = jnp.max(s, axis=1, keepdims=True)
        carry.append((m, weighted(h, jnp.exp(s - m), diag)))

    def body(j, carry):
        rows = pl.ds(pl.multiple_of(j * tq, tq), tq)
        out = []
        for h in heads:
            m, acc = carry[h]
            s = scores(h, rows)
            m_new = jnp.maximum(m, jnp.max(s, axis=1, keepdims=True))
            acc = jnp.exp(m - m_new) * acc + weighted(h, jnp.exp(s - m_new), rows)
            out.append((m_new, acc))
        return tuple(out)

    carry = lax.fori_loop(0, i, body, tuple(carry))
    lp = lam_ref[...]
    lam = (jnp.exp(jnp.sum(lp[0:1] * lp[1:2], axis=1, keepdims=True))
           - jnp.exp(jnp.sum(lp[2:3] * lp[3:4], axis=1, keepdims=True)) + lam_init)
    for h in heads:
        _, acc = carry[h]
        o = acc[:, :LANES] / acc[:, LANES:]
        o = o[:tq] - lam * o[tq:]
        o_ref[:, h * LANES:(h + 1) * LANES] = (_rms_scale(o) * (1.0 - lam_init)).astype(o_ref.dtype)


def _diff_attn(lam_p, qd, kd, vd, batch, seq, lam_init):
    tq = DIFF_TILE
    nq = seq // tq
    n = batch * seq
    width = DIFF_HEADS * LANES
    return pl.pallas_call(
        functools.partial(_diff_attn_kernel, lam_init=lam_init),
        grid=(batch, nq),
        in_specs=[pl.BlockSpec((4, HEAD_DIM), lambda b, i: (0, 0)),
                  pl.BlockSpec((tq, width), lambda b, i: (b * nq + i, 0)),
                  pl.BlockSpec((seq, width), lambda b, i: (b, 0)),
                  pl.BlockSpec((seq, width), lambda b, i: (b, 0))],
        out_specs=pl.BlockSpec((tq, width), lambda b, i: (b * nq + i, 0)),
        out_shape=jax.ShapeDtypeStruct((n, width), _BF16),
        compiler_params=_params(2),
        name="diff_attn",
    )(lam_p, qd, kd, vd)


def _nsa_kernel(q_ref, gate_ref, ov_ref, blk_ref, kc_ref, vc_ref, ks_ref, vs_ref, kw_ref, vw_ref, o_ref):
    i = pl.program_id(1)
    tq = NSA_TILE
    hpg = NSA_HPG
    groups = range(NSA_GROUPS)
    lane = lax.broadcasted_iota(jnp.int32, (1, LANES), 1)
    low = lane < HEAD_DIM
    q0 = i * tq
    qpos1 = q0 + lax.broadcasted_iota(jnp.int32, (tq, 1), 0)

    def per_head(x):
        return jnp.concatenate([x] * hpg, axis=0)

    def group_cols(ref, rows, g):
        return ref[rows, g * LANES:(g + 1) * LANES]

    def v_one(v):
        return jnp.where(low, v, jnp.ones_like(v))

    def stack_q(g):
        parts = []
        for j in range(hpg):
            c0 = g * hpg * HEAD_DIM + (j // 2) * LANES
            qc = q_ref[:, c0:c0 + LANES]
            keep = low if j % 2 == 0 else jnp.logical_not(low)
            parts.append(jnp.where(keep, qc, jnp.zeros_like(qc)))
        return jnp.concatenate(parts, axis=0)

    def compressed_and_select(g, qs):
        kc = kc_ref[0, 0, :, g * LANES:(g + 1) * LANES]
        vc = vc_ref[0, 0, :, g * LANES:(g + 1) * LANES]
        ncp = kc.shape[0]
        s_c = _dot_nt(qs, kc)
        cmp_end = lax.broadcasted_iota(jnp.int32, (1, ncp), 1) * CMP_STRIDE + (CMP_BLOCK - 1)
        s_c = s_c + per_head(_mask_bias(cmp_end <= qpos1))
        mx = jnp.max(s_c, axis=1, keepdims=True)
        mx = jnp.where(mx == -jnp.inf, 0.0, mx)
        p_c = jnp.exp(s_c - mx)
        p_c = p_c / jnp.maximum(jnp.sum(p_c, axis=1, keepdims=True), 1e-30)
        o_c = _dot(p_c.astype(_BF16), vc)

        p_sum = p_c[0:tq]
        for j in range(1, hpg):
            p_sum = p_sum + p_c[j * tq:(j + 1) * tq]
        ov = ov_ref[...]
        p_hi = p_sum.astype(_BF16)
        r1 = p_sum - p_hi.astype(_F32)
        p_mid = r1.astype(_BF16)
        p_lo = (r1 - p_mid.astype(_F32)).astype(_BF16)
        imp = _dot(p_hi, ov) + _dot(p_mid, ov) + _dot(p_lo, ov)

        jsel = lax.broadcasted_iota(jnp.int32, (1, SEL_PAD), 1)
        cur = qpos1 >> SEL_SHIFT
        forced = (jsel == 0) | (jsel == cur) | (jsel == cur - 1)
        future = jsel * SEL_BLOCK > qpos1
        work = jnp.where(future, -1.0, jnp.where(forced, FORCED_SCORE, imp))

        jsel_f = jsel.astype(_F32)
        chosen = jnp.zeros((tq, SEL_PAD), _F32)
        for _ in range(SEL_TOPK):
            best = jnp.max(work, axis=1, keepdims=True)
            first = jnp.min(jnp.where(work == best, jsel_f, float(SEL_PAD)), axis=1, keepdims=True)
            pick = jsel_f == first
            chosen = jnp.where(pick, 1.0, chosen)
            work = jnp.where(pick, -jnp.inf, work)
        return o_c, chosen

    qs, o_c, q_sel = [], [], []
    for g in groups:
        qs.append(stack_q(g))
        oc, chosen = compressed_and_select(g, qs[g])
        o_c.append(oc)
        penalty = ((chosen - 1.0) * MASKED_SCORE).astype(_BF16)
        q_sel.append(jnp.concatenate([qs[g], per_head(penalty)], axis=1))

    def sel_scores(g, rows):
        k_blk = jnp.concatenate([group_cols(ks_ref, rows, g), blk_ref[rows, :]], axis=1)
        return _dot_nt(q_sel[g], k_blk)

    def sel_weighted(g, p, rows):
        return _dot(p.astype(_BF16), v_one(group_cols(vs_ref, rows, g)))

    r_t = lax.broadcasted_iota(jnp.int32, (tq, tq), 0)
    c_t = lax.broadcasted_iota(jnp.int32, (tq, tq), 1)
    tri = per_head(_mask_bias(c_t <= r_t))
    diag = pl.ds(pl.multiple_of(q0, tq), tq)
    carry = []
    for g in groups:
        s = sel_scores(g, diag) + tri
        m = jnp.max(s, axis=1, keepdims=True)
        carry.append((m, sel_weighted(g, jnp.exp(s - m), diag)))

    def body(j, carry):
        rows = pl.ds(pl.multiple_of(j * tq, tq), tq)
        out = []
        for g in groups:
            m, acc = carry[g]
            s = sel_scores(g, rows)
            m_new = jnp.maximum(m, jnp.max(s, axis=1, keepdims=True))
            acc = jnp.exp(m - m_new) * acc + sel_weighted(g, jnp.exp(s - m_new), rows)
            out.append((m_new, acc))
        return tuple(out)

    carry = lax.fori_loop(0, i, body, tuple(carry))

    wlen = WINDOW + tq
    w0 = jnp.maximum(q0 - WINDOW, 0)
    wrows = pl.ds(pl.multiple_of(w0, tq), wlen)
    shift = q0 - w0
    r_w = lax.broadcasted_iota(jnp.int32, (tq, wlen), 0) + shift
    c_w = lax.broadcasted_iota(jnp.int32, (tq, wlen), 1)
    band = per_head(jnp.where(c_w <= r_w, jnp.where(c_w > r_w - WINDOW, 0.0, -jnp.inf), -jnp.inf).astype(_F32))
    res_w = []
    for g in groups:
        s = _dot_nt(qs[g], group_cols(kw_ref, wrows, g)) + band
        m = jnp.max(s, axis=1, keepdims=True)
        res_w.append(_dot(jnp.exp(s - m).astype(_BF16), v_one(group_cols(vw_ref, wrows, g))))

    def normalised(res, sl, high):
        x = res[sl]
        swapped = pltpu.roll(x, HEAD_DIM, axis=1)
        return swapped / x if high else x / swapped

    for g in groups:
        gate = gate_ref[:, g * LANES:(g + 1) * LANES]
        res_s = carry[g][1]
        heads = []
        for j in range(hpg):
            sl = slice(j * tq, (j + 1) * tq)
            high = j % 2 == 1
            heads.append(gate[:, 3 * j:3 * j + 1] * o_c[g][sl]
                         + gate[:, 3 * j + 1:3 * j + 2] * normalised(res_s, sl, high)
                         + gate[:, 3 * j + 2:3 * j + 3] * normalised(res_w[g], sl, high))
        for c in range(hpg // 2):
            col = (g * (hpg // 2) + c) * LANES
            o_ref[:, col:col + LANES] = jnp.where(low, heads[2 * c], heads[2 * c + 1]).astype(o_ref.dtype)


def _overlap_matrix(seq):
    rows = seq // CMP_STRIDE
    n_cmp = (seq - CMP_BLOCK) // CMP_STRIDE + 1
    n_sel = seq // SEL_BLOCK
    cs = np.arange(rows)[:, None] * CMP_STRIDE
    ss = np.arange(SEL_PAD)[None, :] * SEL_BLOCK
    ov = np.clip(np.minimum(cs + CMP_BLOCK, ss + SEL_BLOCK) - np.maximum(cs, ss), 0, None) / CMP_BLOCK
    ov = ov * (np.arange(rows)[:, None] < n_cmp) * (np.arange(SEL_PAD)[None, :] < n_sel)
    return jnp.asarray(ov, dtype=_BF16)


def _block_onehot(seq):
    hit = (np.arange(seq)[:, None] >> SEL_SHIFT) == np.arange(SEL_PAD)[None, :]
    return jnp.asarray(hit, dtype=_BF16)


def _nsa(qn, gn, ckv, ks, vs, kw, vw, batch, seq):
    tq = NSA_TILE
    nq = seq // tq
    n = batch * seq
    rows = seq // CMP_STRIDE
    width = NSA_GROUPS * LANES
    assert seq // SEL_BLOCK <= SEL_PAD and WINDOW % tq == 0 and seq >= WINDOW + tq
    qmap = lambda b, i: (b * nq + i, 0)
    kvmap = lambda b, i: (b, 0)
    return pl.pallas_call(
        _nsa_kernel,
        grid=(batch, nq),
        in_specs=[pl.BlockSpec((tq, NSA_HEADS * HEAD_DIM), qmap),
                  pl.BlockSpec((tq, width), qmap),
                  _resident((rows, SEL_PAD), lambda b, i: (0, 0)),
                  _resident((seq, SEL_PAD), lambda b, i: (0, 0)),
                  pl.BlockSpec((1, 1, rows, width), lambda b, i: (0, b, 0, 0)),
                  pl.BlockSpec((1, 1, rows, width), lambda b, i: (1, b, 0, 0)),
                  pl.BlockSpec((seq, width), kvmap), pl.BlockSpec((seq, width), kvmap),
                  pl.BlockSpec((seq, width), kvmap), pl.BlockSpec((seq, width), kvmap)],
        out_specs=pl.BlockSpec((tq, NSA_HEADS * HEAD_DIM), qmap),
        out_shape=jax.ShapeDtypeStruct((n, NSA_HEADS * HEAD_DIM), _BF16),
        compiler_params=_params(2),
        name="nsa",
    )(qn, gn, _overlap_matrix(seq), _block_onehot(seq), ckv, ckv, ks, vs, kw, vw)


def _merge_kernel(h_ref, oa_ref, ob_ref, ga_ref, gb_ref, wa_ref, wb_ref, wo_ref, o_ref):
    y = (ga_ref[...].astype(_F32) * _dot(oa_ref[...], wa_ref[...])
         + gb_ref[...].astype(_F32) * _dot(ob_ref[...], wb_ref[...]))
    o_ref[...] = h_ref[...] + _dot(y.astype(_BF16), wo_ref[...])


def _merge(h, oa, ob, ga, gb, wa, wb, wo):
    n = h.shape[0]
    tm = min(TOKEN_TILE, n)

    def row(width):
        return pl.BlockSpec((tm, width), lambda i: (i, 0))

    return pl.pallas_call(
        _merge_kernel,
        grid=(n // tm,),
        in_specs=[row(D_MODEL), row(512), row(512), row(D_MODEL), row(D_MODEL),
                  _resident((512, D_MODEL), lambda i: (0, 0)),
                  _resident((512, D_MODEL), lambda i: (0, 0)),
                  _resident((D_MODEL, D_MODEL), lambda i: (0, 0))],
        out_specs=row(D_MODEL),
        out_shape=jax.ShapeDtypeStruct((n, D_MODEL), _F32),
        compiler_params=_params(1),
        name="merge",
    )(h, oa, ob, ga, gb, wa, wb, wo)


def _dup_groups(w):
    a, b = w[:, :HEAD_DIM], w[:, HEAD_DIM:]
    return jnp.concatenate([a, a, b, b], axis=1)


def _prep_w_in(w):
    col = lambda k: w[:, _OFFS[k]:_OFFS[k + 1]]
    q_d, k_d, v_d, q_n, kc, vc, ks, vs, kw, vw, g_n, g_a, g_b = [col(k) for k in range(13)]
    per_group = NSA_HPG * 3
    pad = jnp.zeros((w.shape[0], LANES - per_group), w.dtype)
    gn = jnp.concatenate([g_n[:, :per_group], pad, g_n[:, per_group:], pad], axis=1)
    out = jnp.concatenate([q_d, k_d, q_n, _dup_groups(ks), _dup_groups(kw),
                           v_d, _dup_groups(vs), _dup_groups(vw), kc, vc, gn, g_a, g_b], axis=1)
    assert out.shape[1] == _C_TOTAL
    return out.astype(_BF16)


def _prep_compress(pos, w1, w2):
    half = CMP_BLOCK // 2
    w1 = w1.reshape(2, 2, half, HEAD_DIM, CMP_HIDDEN)
    zeros = jnp.zeros_like(w1)
    per_group = []
    for g in range(NSA_GROUPS):
        lanes = [zeros, zeros]
        lanes[g] = w1
        per_group.append(jnp.concatenate(lanes, axis=3))
    w1_x = jnp.stack(per_group, axis=1).reshape(2, NSA_GROUPS, 2, half * LANES, CMP_HIDDEN).astype(_BF16)
    pos = pos.reshape(2, 2, half, HEAD_DIM)
    pos_x = jnp.concatenate([pos, pos], axis=3).reshape(2, 2, 1, half * LANES).astype(_F32)
    w2_x = jnp.concatenate([w2, w2], axis=2).astype(_BF16)
    return pos_x, w1_x, w2_x


def kernel(x, positions, ffn1_norm, ffn1_w_gu, ffn1_w_down, mix_norm, w_in, diff_lambda, cmp_pos, cmp_w1, cmp_w2, w_branch_a, w_branch_b, w_out, ffn2_norm, ffn2_w_gu, ffn2_w_down, final_norm):
    batch, seq, _ = x.shape
    n = batch * seq
    depth = w_in.shape[0]
    assert seq % max(NSA_TILE, DIFF_TILE, CMP_STRIDE * 8) == 0
    cos_t, sin_t = _rope_tables(positions)
    h = x.reshape(n, D_MODEL)
    for l in range(depth):
        bf = lambda a: a.astype(_BF16)
        h = _ffn(h, ffn1_norm[l], bf(ffn1_w_gu[l][:, :D_FF]), bf(ffn1_w_gu[l][:, D_FF:]), bf(ffn1_w_down[l]))
        (qd, kd, qn, ks, kw, vd, vs, vw, craw, gn, ga, gb) = _mix_in(h, mix_norm[l], cos_t, sin_t, _prep_w_in(w_in[l]))
        ckv = _compress(craw, *_prep_compress(cmp_pos[l], cmp_w1[l], cmp_w2[l]), batch, seq)
        lam_init = 0.8 - 0.6 * math.exp(-0.3 * l)
        o_a = _diff_attn(diff_lambda[l].astype(_F32), qd, kd, vd, batch, seq, lam_init)
        o_b = _nsa(qn, gn, ckv, ks, vs, kw, vw, batch, seq)
        h = _merge(h, o_a, o_b, ga, gb, bf(w_branch_a[l]), bf(w_branch_b[l]), bf(w_out[l]))
        last = l == depth - 1
        h = _ffn(h, ffn2_norm[l], bf(ffn2_w_gu[l][:, :D_FF]), bf(ffn2_w_gu[l][:, D_FF:]), bf(ffn2_w_down[l]),
                 final_g=final_norm if last else None)
    return h.reshape(batch, seq, D_MODEL)
```

```python
import functools
import math

import numpy as np
import jax
import jax.numpy as jnp
from jax import lax
from jax.experimental import pallas as pl
from jax.experimental.pallas import tpu as pltpu

D_MODEL = 1024
D_FF = 2816
HEAD_DIM = 64
ROPE_DIM = HEAD_DIM // 4
ROPE_HALF = ROPE_DIM // 2
ROPE_THETA = 500000.0
EPS = 1e-6
DIFF_HEADS = 4
NSA_HEADS = 8
NSA_GROUPS = 2
NSA_HPG = NSA_HEADS // NSA_GROUPS
CMP_BLOCK = 32
CMP_STRIDE = 16
CMP_HIDDEN = 256
SEL_BLOCK = 64
SEL_SHIFT = SEL_BLOCK.bit_length() - 1
SEL_TOPK = 8
WINDOW = 512
FORCED_SCORE = 1.0e4
MASKED_SCORE = 2.0 ** 100
ATTN_SCALE = HEAD_DIM ** -0.5

_SPLITS = (512, 512, 512, 512, 128, 128, 128, 128, 128, 128, 24, 1024, 1024)
_OFFS = np.concatenate([[0], np.cumsum(_SPLITS)]).tolist()

LANES = 128
VMEM_LIMIT_BYTES = 56 * 1024 * 1024
TOKEN_TILE = 512
DIFF_TILE = 256
NSA_TILE = 256
SEL_PAD = 128

_C_QD, _C_KD, _C_QN, _C_KS, _C_KW = 0, 512, 1024, 1536, 1792
_C_ROPE_END = 2048
_C_CRAW, _C_GN, _C_GA, _C_GB = 2048, 2304, 2560, 3584
_C_TOTAL = 4608
_VT_DIFF_ROWS = 512
_VT_NSA_ROWS = 2 * NSA_GROUPS * HEAD_DIM
ONES_ROWS = 16

_BF16 = jnp.bfloat16
_F32 = jnp.float32


def _params(n_grid):
    return pltpu.CompilerParams(
        dimension_semantics=("arbitrary",) * n_grid,
        vmem_limit_bytes=VMEM_LIMIT_BYTES,
    )


def _resident(shape, index_map):
    return pl.BlockSpec(shape, index_map, pipeline_mode=pl.Buffered(1))


def _dot(a, b):
    return jnp.dot(a, b, preferred_element_type=_F32)


def _dot_nt(a, b):
    return lax.dot_general(a, b, (((1,), (1,)), ((), ())), preferred_element_type=_F32)


def _sigmoid(x):
    return 1.0 / (1.0 + jnp.exp(-x))


def _rms_scale(x):
    return x * lax.rsqrt(jnp.mean(x * x, axis=-1, keepdims=True) + EPS)


def _rope_table_kernel(pos_ref, cos_ref, sin_ref):
    lane = lax.broadcasted_iota(jnp.int32, (1, LANES), 1)
    in_head = lane & (HEAD_DIM - 1)
    freq = (in_head & (ROPE_HALF - 1)).astype(_F32)
    inv = jnp.exp(freq * (-2.0 / ROPE_DIM * math.log(ROPE_THETA)))
    inv = jnp.where(in_head < ROPE_DIM, inv, 0.0)
    ang = pos_ref[...].astype(_F32) * inv
    cos_ref[...] = jnp.cos(ang)
    sin = jnp.sin(ang)
    sin_ref[...] = jnp.where(in_head < ROPE_HALF, -sin, sin)


def _rope_tables(positions):
    n = positions.size
    tm = min(TOKEN_TILE, n)
    pos = positions.reshape(n, 1)
    return pl.pallas_call(
        _rope_table_kernel,
        grid=(n // tm,),
        in_specs=[pl.BlockSpec((tm, 1), lambda i: (i, 0))],
        out_specs=[pl.BlockSpec((tm, LANES), lambda i: (i, 0))] * 2,
        out_shape=[jax.ShapeDtypeStruct((n, LANES), _F32)] * 2,
        compiler_params=_params(1),
        name="rope_tables",
    )(pos)


def _ffn_kernel(h_ref, g_ref, wg_ref, wu_ref, wd_ref, *rest, final):
    if final:
        fg_ref, o_ref = rest
    else:
        (o_ref,) = rest
    x = h_ref[...]
    u = (_rms_scale(x) * g_ref[...]).astype(_BF16)
    gate = _dot(u, wg_ref[...])
    up = _dot(u, wu_ref[...])
    act = (gate * _sigmoid(gate) * up).astype(_BF16)
    y = x + 0.5 * _dot(act, wd_ref[...])
    if final:
        y = _rms_scale(y) * fg_ref[...]
    o_ref[...] = y


def _ffn(h, g, wg, wu, wd, final_g=None):
    n = h.shape[0]
    tm = min(TOKEN_TILE, n)
    final = final_g is not None
    row = pl.BlockSpec((tm, D_MODEL), lambda i: (i, 0))
    vec = _resident((1, D_MODEL), lambda i: (0, 0))
    in_specs = [row, vec,
                _resident((D_MODEL, D_FF), lambda i: (0, 0)),
                _resident((D_MODEL, D_FF), lambda i: (0, 0)),
                _resident((D_FF, D_MODEL), lambda i: (0, 0))]
    args = [h, g.reshape(1, D_MODEL), wg, wu, wd]
    if final:
        in_specs.append(vec)
        args.append(final_g.reshape(1, D_MODEL))
    return pl.pallas_call(
        functools.partial(_ffn_kernel, final=final),
        grid=(n // tm,),
        in_specs=in_specs,
        out_specs=row,
        out_shape=jax.ShapeDtypeStruct((n, D_MODEL), _F32),
        compiler_params=_params(1),
        name="ffn_final" if final else "ffn",
    )(*args)


def _mix_in_kernel(h_ref, g_ref, cos_ref, sin_ref, w_ref, wvt_ref,
                   qd_ref, kd_ref, qn_ref, ks_ref, kw_ref, vt_ref,
                   craw_ref, gn_ref, ga_ref, gb_ref):
    x = h_ref[...]
    u = (_rms_scale(x) * g_ref[...]).astype(_BF16)
    cos = cos_ref[...]
    sin = sin_ref[...]
    lane = lax.broadcasted_iota(jnp.int32, (1, LANES), 1)
    first_half = (lane & (HEAD_DIM - 1)) < ROPE_HALF

    def proj(c0, width):
        return _dot(u, w_ref[:, c0:c0 + width])

    def rope_store(o_ref, c0, width, scale):
        z = proj(c0, width)
        for c in range(width // LANES):
            zc = z[:, c * LANES:(c + 1) * LANES]
            partner = jnp.where(first_half, pltpu.roll(zc, LANES - ROPE_HALF, axis=1),
                                pltpu.roll(zc, ROPE_HALF, axis=1))
            r = zc * cos + partner * sin
            if scale != 1.0:
                r = r * scale
            o_ref[:, c * LANES:(c + 1) * LANES] = r.astype(o_ref.dtype)

    rope_store(qd_ref, _C_QD, 512, ATTN_SCALE)
    rope_store(kd_ref, _C_KD, 512, 1.0)
    rope_store(qn_ref, _C_QN, 512, ATTN_SCALE)
    rope_store(ks_ref, _C_KS, 256, 1.0)
    rope_store(kw_ref, _C_KW, 256, 1.0)
    vt_ref[...] = _dot_nt(wvt_ref[...], u).astype(_BF16)
    craw = proj(_C_CRAW, 256)
    craw_ref[0] = craw[:, :LANES]
    craw_ref[1] = craw[:, LANES:]
    gn_ref[...] = _sigmoid(proj(_C_GN, 256))
    ga_ref[...] = _sigmoid(proj(_C_GA, D_MODEL)).astype(_BF16)
    gb_ref[...] = _sigmoid(proj(_C_GB, D_MODEL)).astype(_BF16)


def _mix_in(h, g, cos_t, sin_t, w, wvt):
    n = h.shape[0]
    tm = min(TOKEN_TILE, n)
    vd_width = wvt.shape[0]

    def row(width):
        return pl.BlockSpec((tm, width), lambda i: (i, 0))

    def bf16_rows(width):
        return row(width), jax.ShapeDtypeStruct((n, width), _BF16)

    outs = [bf16_rows(512), bf16_rows(512), bf16_rows(512), bf16_rows(256), bf16_rows(256),
            (pl.BlockSpec((vd_width, tm), lambda i: (0, i)), jax.ShapeDtypeStruct((vd_width, n), _BF16))]
    out_specs = [spec for spec, _ in outs]
    out_shape = [shape for _, shape in outs]
    out_specs += [pl.BlockSpec((2, tm, LANES), lambda i: (0, i, 0)), row(256), row(D_MODEL), row(D_MODEL)]
    out_shape += [jax.ShapeDtypeStruct((2, n, LANES), _F32),
                  jax.ShapeDtypeStruct((n, 256), _F32),
                  jax.ShapeDtypeStruct((n, D_MODEL), _BF16),
                  jax.ShapeDtypeStruct((n, D_MODEL), _BF16)]
    return pl.pallas_call(
        _mix_in_kernel,
        grid=(n // tm,),
        in_specs=[row(D_MODEL), _resident((1, D_MODEL), lambda i: (0, 0)),
                  row(LANES), row(LANES),
                  _resident((D_MODEL, _C_TOTAL), lambda i: (0, 0)),
                  _resident((vd_width, D_MODEL), lambda i: (0, 0))],
        out_specs=out_specs,
        out_shape=out_shape,
        compiler_params=_params(1),
        name="mix_in",
    )(h, g.reshape(1, D_MODEL), cos_t, sin_t, w, wvt)


def _compress_kernel(x_ref, pos_ref, w1_ref, w2_ref, o_ref, *, n_cmp):
    x = x_ref[0, 0]
    rows = x.shape[0]
    x_top = (x + pos_ref[0, 0]).astype(_BF16)
    x_bot = (x + pos_ref[0, 1]).astype(_BF16)
    valid = lax.broadcasted_iota(jnp.int32, (rows, 1), 0) < n_cmp
    for g in range(NSA_GROUPS):
        top = _dot(x_top, w1_ref[0, g, 0])
        bot = _dot(x_bot, w1_ref[0, g, 1])
        pre = top + pltpu.roll(bot, rows - 1, axis=0)
        hid = (pre * _sigmoid(pre)).astype(_BF16)
        out = _dot(hid, w2_ref[0])
        out = jnp.where(valid, out, 0.0)
        o_ref[0, 0, :, g * LANES:(g + 1) * LANES] = out.astype(o_ref.dtype)


def _compress(craw, pos_x, w1_x, w2_x, batch, seq):
    rows = seq // CMP_STRIDE
    n_cmp = (seq - CMP_BLOCK) // CMP_STRIDE + 1
    width = CMP_STRIDE * LANES
    x = craw.reshape(2, batch, rows, width)
    return pl.pallas_call(
        functools.partial(_compress_kernel, n_cmp=n_cmp),
        grid=(2, batch),
        in_specs=[pl.BlockSpec((1, 1, rows, width), lambda k, b: (k, b, 0, 0)),
                  pl.BlockSpec((1, 2, 1, width), lambda k, b: (k, 0, 0, 0)),
                  pl.BlockSpec((1, NSA_GROUPS, 2, width, CMP_HIDDEN), lambda k, b: (k, 0, 0, 0, 0)),
                  pl.BlockSpec((1, CMP_HIDDEN, LANES), lambda k, b: (k, 0, 0))],
        out_specs=pl.BlockSpec((1, 1, rows, NSA_GROUPS * LANES), lambda k, b: (k, b, 0, 0)),
        out_shape=jax.ShapeDtypeStruct((2, batch, rows, NSA_GROUPS * LANES), _BF16),
        compiler_params=_params(2),
        name="compress",
    )(x, pos_x, w1_x, w2_x)


def _mask_bias(mask):
    return jnp.where(mask, 0.0, -jnp.inf).astype(_F32)


def _diff_attn_kernel(lam_ref, q_ref, k_ref, vt_ref, o_ref, *, lam_init):
    i = pl.program_id(1)
    tq = DIFF_TILE
    lane = lax.broadcasted_iota(jnp.int32, (1, LANES), 1)
    ones_rows = jnp.ones((ONES_ROWS, tq), _BF16)
    heads = range(DIFF_HEADS)

    qs = []
    for h in heads:
        q = q_ref[:, h * LANES:(h + 1) * LANES]
        zero = jnp.zeros_like(q)
        qs.append(jnp.concatenate([jnp.where(lane < HEAD_DIM, q, zero), jnp.where(lane >= HEAD_DIM, q, zero)], axis=0))

    def scores(h, keys):
        return _dot_nt(k_ref[keys, h * LANES:(h + 1) * LANES], qs[h])

    def weighted(h, p, keys):
        v_one = jnp.concatenate([vt_ref[h * LANES:(h + 1) * LANES, keys], ones_rows], axis=0)
        return _dot(v_one, p)

    def all_scores(j):
        keys = pl.ds(pl.multiple_of(j * tq, tq), tq)
        return tuple(scores(h, keys) for h in heads)

    def chunk(j, s, carry, bias):
        keys = pl.ds(pl.multiple_of(j * tq, tq), tq)
        if bias is not None:
            s = [x + bias for x in s]
        m_new = [jnp.max(x, axis=0, keepdims=True) for x in s]
        if carry is not None:
            m_new = [jnp.maximum(carry[h][0], m_new[h]) for h in heads]
        p = [jnp.exp((s[h] - m_new[h]).astype(_BF16)) for h in heads]
        res = [weighted(h, p[h], keys) for h in heads]
        if carry is not None:
            res = [jnp.exp(carry[h][0] - m_new[h]) * carry[h][1] + res[h] for h in heads]
        return tuple((m_new[h], res[h]) for h in heads)

    key = lax.broadcasted_iota(jnp.int32, (tq, tq), 0)
    qry = lax.broadcasted_iota(jnp.int32, (tq, tq), 1)
    tri = _mask_bias(key <= qry)
    carry = chunk(i, all_scores(i), None, jnp.concatenate([tri, tri], axis=1))
    carry = lax.fori_loop(0, i, lambda j, c: chunk(j, all_scores(j), c, None), carry)
    lp = lam_ref[...]
    lam = (jnp.exp(jnp.sum(lp[0:1] * lp[1:2], axis=1, keepdims=True))
           - jnp.exp(jnp.sum(lp[2:3] * lp[3:4], axis=1, keepdims=True)) + lam_init)
    for h in heads:
        _, acc = carry[h]
        o = acc[:LANES] / acc[LANES:LANES + 1]
        o = o[:, :tq] - lam * o[:, tq:]
        o = o * lax.rsqrt(jnp.mean(o * o, axis=0, keepdims=True) + EPS) * (1.0 - lam_init)
        o_ref[:, h * LANES:(h + 1) * LANES] = o.T.astype(o_ref.dtype)


def _diff_attn(lam_p, qd, kd, vdt, batch, seq, lam_init):
    tq = DIFF_TILE
    nq = seq // tq
    n = batch * seq
    width = DIFF_HEADS * LANES
    return pl.pallas_call(
        functools.partial(_diff_attn_kernel, lam_init=lam_init),
        grid=(batch, nq),
        in_specs=[pl.BlockSpec((4, HEAD_DIM), lambda b, i: (0, 0)),
                  pl.BlockSpec((tq, width), lambda b, i: (b * nq + i, 0)),
                  pl.BlockSpec((seq, width), lambda b, i: (b, 0)),
                  pl.BlockSpec((width, seq), lambda b, i: (0, b))],
        out_specs=pl.BlockSpec((tq, width), lambda b, i: (b * nq + i, 0)),
        out_shape=jax.ShapeDtypeStruct((n, width), _BF16),
        compiler_params=_params(2),
        name="diff_attn",
    )(lam_p, qd, kd, vdt)


def _nsa_kernel(q_ref, gate_ref, ov_ref, blk_ref, kc_ref, vc_ref, ks_ref, kw_ref, vt_ref, o_ref):
    i = pl.program_id(1)
    tq = NSA_TILE
    hpg = NSA_HPG
    groups = range(NSA_GROUPS)
    lane = lax.broadcasted_iota(jnp.int32, (1, LANES), 1)
    low = lane < HEAD_DIM
    q0 = i * tq
    qpos1 = q0 + lax.broadcasted_iota(jnp.int32, (tq, 1), 0)

    def per_head(x):
        return jnp.concatenate([x] * hpg, axis=0)

    def group_cols(ref, rows, g):
        return ref[rows, g * LANES:(g + 1) * LANES]

    def stack_q(g):
        parts = []
        for j in range(hpg):
            c0 = g * hpg * HEAD_DIM + (j // 2) * LANES
            qc = q_ref[:, c0:c0 + LANES]
            keep = low if j % 2 == 0 else jnp.logical_not(low)
            parts.append(jnp.where(keep, qc, jnp.zeros_like(qc)))
        return jnp.concatenate(parts, axis=0)

    def compressed_and_select(g, qs):
        kc = kc_ref[0, 0, :, g * LANES:(g + 1) * LANES]
        vc = vc_ref[0, 0, :, g * LANES:(g + 1) * LANES]
        ncp = kc.shape[0]
        s_c = _dot_nt(qs, kc)
        cmp_end = lax.broadcasted_iota(jnp.int32, (1, ncp), 1) * CMP_STRIDE + (CMP_BLOCK - 1)
        s_c = s_c + per_head(_mask_bias(cmp_end <= qpos1))
        mx = jnp.max(s_c, axis=1, keepdims=True)
        mx = jnp.where(mx == -jnp.inf, 0.0, mx)
        p_c = jnp.exp(s_c - mx)
        p_c = p_c / jnp.maximum(jnp.sum(p_c, axis=1, keepdims=True), 1e-30)
        o_c = _dot(p_c.astype(_BF16), vc)

        p_sum = p_c[0:tq]
        for j in range(1, hpg):
            p_sum = p_sum + p_c[j * tq:(j + 1) * tq]
        ov = ov_ref[...]
        p_hi = p_sum.astype(_BF16)
        r1 = p_sum - p_hi.astype(_F32)
        p_mid = r1.astype(_BF16)
        p_lo = (r1 - p_mid.astype(_F32)).astype(_BF16)
        imp = _dot(p_hi, ov) + _dot(p_mid, ov) + _dot(p_lo, ov)

        jsel = lax.broadcasted_iota(jnp.int32, (1, SEL_PAD), 1)
        cur = qpos1 >> SEL_SHIFT
        forced = (jsel == 0) | (jsel == cur) | (jsel == cur - 1)
        future = jsel * SEL_BLOCK > qpos1
        work = jnp.where(future, -1.0, jnp.where(forced, FORCED_SCORE, imp))

        jsel_f = jsel.astype(_F32)
        chosen = jnp.zeros((tq, SEL_PAD), _F32)
        for _ in range(SEL_TOPK):
            best = jnp.max(work, axis=1, keepdims=True)
            first = jnp.min(jnp.where(work == best, jsel_f, float(SEL_PAD)), axis=1, keepdims=True)
            pick = jsel_f == first
            chosen = jnp.where(pick, 1.0, chosen)
            work = jnp.where(pick, -jnp.inf, work)
        return o_c, chosen

    qs, o_c, q_sel = [], [], []
    for g in groups:
        qs.append(stack_q(g))
        oc, chosen = compressed_and_select(g, qs[g])
        o_c.append(oc)
        penalty = ((chosen - 1.0) * MASKED_SCORE).astype(_BF16)
        q_sel.append(jnp.concatenate([qs[g], per_head(penalty)], axis=1))

    def per_head_cols(x):
        return jnp.concatenate([x] * hpg, axis=1)

    def v_rows(kind, g, keys):
        r0 = (kind * NSA_GROUPS + g) * HEAD_DIM
        ones_rows = jnp.ones((ONES_ROWS, keys.size), _BF16)
        return jnp.concatenate([vt_ref[r0:r0 + HEAD_DIM, keys], ones_rows], axis=0)

    def sel_scores(g, keys):
        k_blk = jnp.concatenate([group_cols(ks_ref, keys, g), blk_ref[keys, :]], axis=1)
        return _dot_nt(k_blk, q_sel[g])

    def sel_chunk(j, carry, bias):
        keys = pl.ds(pl.multiple_of(j * tq, tq), tq)
        s = [sel_scores(g, keys) for g in groups]
        if bias is not None:
            s = [x + bias for x in s]
        m_new = [jnp.max(x, axis=0, keepdims=True) for x in s]
        if carry is not None:
            m_new = [jnp.maximum(carry[g][0], m_new[g]) for g in groups]
        p = [jnp.exp((s[g] - m_new[g]).astype(_BF16)) for g in groups]
        res = [_dot(v_rows(0, g, keys), p[g]) for g in groups]
        if carry is not None:
            res = [jnp.exp(carry[g][0] - m_new[g]) * carry[g][1] + res[g] for g in groups]
        return tuple((m_new[g], res[g]) for g in groups)

    wlen = WINDOW + tq
    w0 = jnp.maximum(q0 - WINDOW, 0)
    wkeys = pl.ds(pl.multiple_of(w0, tq), wlen)
    key_w = lax.broadcasted_iota(jnp.int32, (wlen, tq), 0)
    qry_w = lax.broadcasted_iota(jnp.int32, (wlen, tq), 1) + (q0 - w0)
    band = per_head_cols(
        jnp.where(key_w <= qry_w, jnp.where(key_w > qry_w - WINDOW, 0.0, -jnp.inf), -jnp.inf).astype(_F32))
    s_w = [_dot_nt(group_cols(kw_ref, wkeys, g), qs[g]) + band for g in groups]
    p_w = [jnp.exp((x - jnp.max(x, axis=0, keepdims=True)).astype(_BF16)) for x in s_w]
    res_w = [_dot(v_rows(1, g, wkeys), p_w[g]) for g in groups]

    key_t = lax.broadcasted_iota(jnp.int32, (tq, tq), 0)
    qry_t = lax.broadcasted_iota(jnp.int32, (tq, tq), 1)
    carry = sel_chunk(i, None, per_head_cols(_mask_bias(key_t <= qry_t)))
    carry = lax.fori_loop(0, i, lambda j, c: sel_chunk(j, c, None), carry)

    def normalised_rows(res):
        return res[:HEAD_DIM] / res[HEAD_DIM:HEAD_DIM + 1]

    def head_pair(o_t, c):
        pair = jnp.concatenate([o_t[:, (2 * c) * tq:(2 * c + 1) * tq], o_t[:, (2 * c + 1) * tq:(2 * c + 2) * tq]], axis=0)
        return pair.T

    for g in groups:
        gate = gate_ref[:, g * LANES:(g + 1) * LANES]
        o_s = normalised_rows(carry[g][1])
        o_w = normalised_rows(res_w[g])

        def gate_pair(c, branch):
            j0, j1 = 2 * c, 2 * c + 1
            return jnp.where(low, gate[:, 3 * j0 + branch:3 * j0 + branch + 1],
                             gate[:, 3 * j1 + branch:3 * j1 + branch + 1])

        for c in range(hpg // 2):
            o_cmp = jnp.where(low, o_c[g][(2 * c) * tq:(2 * c + 1) * tq], o_c[g][(2 * c + 1) * tq:(2 * c + 2) * tq])
            out = gate_pair(c, 0) * o_cmp + gate_pair(c, 1) * head_pair(o_s, c) + gate_pair(c, 2) * head_pair(o_w, c)
            col = (g * (hpg // 2) + c) * LANES
            o_ref[:, col:col + LANES] = out.astype(o_ref.dtype)


def _overlap_matrix(seq):
    rows = seq // CMP_STRIDE
    n_cmp = (seq - CMP_BLOCK) // CMP_STRIDE + 1
    n_sel = seq // SEL_BLOCK
    cs = np.arange(rows)[:, None] * CMP_STRIDE
    ss = np.arange(SEL_PAD)[None, :] * SEL_BLOCK
    ov = np.clip(np.minimum(cs + CMP_BLOCK, ss + SEL_BLOCK) - np.maximum(cs, ss), 0, None) / CMP_BLOCK
    ov = ov * (np.arange(rows)[:, None] < n_cmp) * (np.arange(SEL_PAD)[None, :] < n_sel)
    return jnp.asarray(ov, dtype=_BF16)


def _block_onehot(seq):
    hit = (np.arange(seq)[:, None] >> SEL_SHIFT) == np.arange(SEL_PAD)[None, :]
    return jnp.asarray(hit, dtype=_BF16)


def _nsa(qn, gn, ckv, ks, kw, vt, batch, seq):
    tq = NSA_TILE
    nq = seq // tq
    n = batch * seq
    rows = seq // CMP_STRIDE
    width = NSA_GROUPS * LANES
    assert seq // SEL_BLOCK <= SEL_PAD and WINDOW % tq == 0 and seq >= WINDOW + tq
    assert _VT_DIFF_ROWS % _VT_NSA_ROWS == 0
    qmap = lambda b, i: (b * nq + i, 0)
    kvmap = lambda b, i: (b, 0)
    return pl.pallas_call(
        _nsa_kernel,
        grid=(batch, nq),
        in_specs=[pl.BlockSpec((tq, NSA_HEADS * HEAD_DIM), qmap),
                  pl.BlockSpec((tq, width), qmap),
                  _resident((rows, SEL_PAD), lambda b, i: (0, 0)),
                  _resident((seq, SEL_PAD), lambda b, i: (0, 0)),
                  pl.BlockSpec((1, 1, rows, width), lambda b, i: (0, b, 0, 0)),
                  pl.BlockSpec((1, 1, rows, width), lambda b, i: (1, b, 0, 0)),
                  pl.BlockSpec((seq, width), kvmap), pl.BlockSpec((seq, width), kvmap),
                  pl.BlockSpec((_VT_NSA_ROWS, seq), lambda b, i: (_VT_DIFF_ROWS // _VT_NSA_ROWS, b))],
        out_specs=pl.BlockSpec((tq, NSA_HEADS * HEAD_DIM), qmap),
        out_shape=jax.ShapeDtypeStruct((n, NSA_HEADS * HEAD_DIM), _BF16),
        compiler_params=_params(2),
        name="nsa",
    )(qn, gn, _overlap_matrix(seq), _block_onehot(seq), ckv, ckv, ks, kw, vt)


def _merge_kernel(h_ref, oa_ref, ob_ref, ga_ref, gb_ref, wa_ref, wb_ref, wo_ref, o_ref):
    y = (ga_ref[...].astype(_F32) * _dot(oa_ref[...], wa_ref[...])
         + gb_ref[...].astype(_F32) * _dot(ob_ref[...], wb_ref[...]))
    o_ref[...] = h_ref[...] + _dot(y.astype(_BF16), wo_ref[...])


def _merge(h, oa, ob, ga, gb, wa, wb, wo):
    n = h.shape[0]
    tm = min(TOKEN_TILE, n)

    def row(width):
        return pl.BlockSpec((tm, width), lambda i: (i, 0))

    return pl.pallas_call(
        _merge_kernel,
        grid=(n // tm,),
        in_specs=[row(D_MODEL), row(512), row(512), row(D_MODEL), row(D_MODEL),
                  _resident((512, D_MODEL), lambda i: (0, 0)),
                  _resident((512, D_MODEL), lambda i: (0, 0)),
                  _resident((D_MODEL, D_MODEL), lambda i: (0, 0))],
        out_specs=row(D_MODEL),
        out_shape=jax.ShapeDtypeStruct((n, D_MODEL), _F32),
        compiler_params=_params(1),
        name="merge",
    )(h, oa, ob, ga, gb, wa, wb, wo)


def _dup_groups(w):
    a, b = w[:, :HEAD_DIM], w[:, HEAD_DIM:]
    return jnp.concatenate([a, a, b, b], axis=1)


def _prep_w_in(w):
    col = lambda k: w[:, _OFFS[k]:_OFFS[k + 1]]
    q_d, k_d, v_d, q_n, kc, vc, ks, vs, kw, vw, g_n, g_a, g_b = [col(k) for k in range(13)]
    per_group = NSA_HPG * 3
    pad = jnp.zeros((w.shape[0], LANES - per_group), w.dtype)
    gn = jnp.concatenate([g_n[:, :per_group], pad, g_n[:, per_group:], pad], axis=1)
    out = jnp.concatenate([q_d, k_d, q_n, _dup_groups(ks), _dup_groups(kw), kc, vc, gn, g_a, g_b], axis=1)
    assert out.shape[1] == _C_TOTAL
    w_vt = jnp.concatenate([v_d, vs, vw], axis=1).T
    assert w_vt.shape[0] == _VT_DIFF_ROWS + _VT_NSA_ROWS
    return out.astype(_BF16), w_vt.astype(_BF16)


def _prep_compress(pos, w1, w2):
    half = CMP_BLOCK // 2
    w1 = w1.reshape(2, 2, half, HEAD_DIM, CMP_HIDDEN)
    zeros = jnp.zeros_like(w1)
    per_group = []
    for g in range(NSA_GROUPS):
        lanes = [zeros, zeros]
        lanes[g] = w1
        per_group.append(jnp.concatenate(lanes, axis=3))
    w1_x = jnp.stack(per_group, axis=1).reshape(2, NSA_GROUPS, 2, half * LANES, CMP_HIDDEN).astype(_BF16)
    pos = pos.reshape(2, 2, half, HEAD_DIM)
    pos_x = jnp.concatenate([pos, pos], axis=3).reshape(2, 2, 1, half * LANES).astype(_F32)
    w2_x = jnp.concatenate([w2, w2], axis=2).astype(_BF16)
    return pos_x, w1_x, w2_x


def kernel(x, positions, ffn1_norm, ffn1_w_gu, ffn1_w_down, mix_norm, w_in, diff_lambda, cmp_pos, cmp_w1, cmp_w2, w_branch_a, w_branch_b, w_out, ffn2_norm, ffn2_w_gu, ffn2_w_down, final_norm):
    batch, seq, _ = x.shape
    n = batch * seq
    depth = w_in.shape[0]
    assert seq % max(NSA_TILE, DIFF_TILE, CMP_STRIDE * 8) == 0
    cos_t, sin_t = _rope_tables(positions)
    h = x.reshape(n, D_MODEL)
    for l in range(depth):
        bf = lambda a: a.astype(_BF16)
        h = _ffn(h, ffn1_norm[l], bf(ffn1_w_gu[l][:, :D_FF]), bf(ffn1_w_gu[l][:, D_FF:]), bf(ffn1_w_down[l]))
        (qd, kd, qn, ks, kw, vt, craw, gn, ga, gb) = _mix_in(h, mix_norm[l], cos_t, sin_t, *_prep_w_in(w_in[l]))
        ckv = _compress(craw, *_prep_compress(cmp_pos[l], cmp_w1[l], cmp_w2[l]), batch, seq)
        lam_init = 0.8 - 0.6 * math.exp(-0.3 * l)
        o_a = _diff_attn(diff_lambda[l].astype(_F32), qd, kd, vt, batch, seq, lam_init)
        o_b = _nsa(qn, gn, ckv, ks, kw, vt, batch, seq)
        h = _merge(h, o_a, o_b, ga, gb, bf(w_branch_a[l]), bf(w_branch_b[l]), bf(w_out[l]))
        last = l == depth - 1
        h = _ffn(h, ffn2_norm[l], bf(ffn2_w_gu[l][:, :D_FF]), bf(ffn2_w_gu[l][:, D_FF:]), bf(ffn2_w_down[l]),
                 final_g=final_norm if last else None)
    return h.reshape(batch, seq, D_MODEL)
```

```python
import functools
import math

import numpy as np
import jax
import jax.numpy as jnp
from jax import lax
from jax.experimental import pallas as pl
from jax.experimental.pallas import tpu as pltpu

D_MODEL = 1024
D_FF = 2816
HEAD_DIM = 64
ROPE_DIM = HEAD_DIM // 4
ROPE_HALF = ROPE_DIM // 2
ROPE_THETA = 500000.0
EPS = 1e-6
DIFF_HEADS = 4
NSA_HEADS = 8
NSA_GROUPS = 2
NSA_HPG = NSA_HEADS // NSA_GROUPS
CMP_BLOCK = 32
CMP_STRIDE = 16
CMP_HIDDEN = 256
SEL_BLOCK = 64
SEL_SHIFT = SEL_BLOCK.bit_length() - 1
SEL_TOPK = 8
WINDOW = 512
FORCED_SCORE = 1.0e4
MASKED_SCORE = 2.0 ** 100
ATTN_SCALE = HEAD_DIM ** -0.5

_SPLITS = (512, 512, 512, 512, 128, 128, 128, 128, 128, 128, 24, 1024, 1024)
_OFFS = np.concatenate([[0], np.cumsum(_SPLITS)]).tolist()

LANES = 128
VMEM_LIMIT_BYTES = 56 * 1024 * 1024
TOKEN_TILE = 512
DIFF_TILE = 256
NSA_TILE = 256
SEL_PAD = 128

_C_QD, _C_KD, _C_QN, _C_KS, _C_KW = 0, 512, 1024, 1536, 1792
_C_ROPE_END = 2048
_C_CRAW, _C_GN, _C_GA, _C_GB = 2048, 2304, 2560, 3584
_C_TOTAL = 4608
_VT_DIFF_ROWS = 512
_VT_NSA_ROWS = 2 * NSA_GROUPS * HEAD_DIM
ONES_ROWS = 16

_BF16 = jnp.bfloat16
_F32 = jnp.float32


def _params(n_grid):
    return pltpu.CompilerParams(
        dimension_semantics=("arbitrary",) * n_grid,
        vmem_limit_bytes=VMEM_LIMIT_BYTES,
    )


def _resident(shape, index_map):
    return pl.BlockSpec(shape, index_map, pipeline_mode=pl.Buffered(1))


def _dot(a, b):
    return jnp.dot(a, b, preferred_element_type=_F32)


def _dot_nt(a, b):
    return lax.dot_general(a, b, (((1,), (1,)), ((), ())), preferred_element_type=_F32)


def _sigmoid(x):
    return 1.0 / (1.0 + jnp.exp(-x))


def _rms_scale(x):
    return x * lax.rsqrt(jnp.mean(x * x, axis=-1, keepdims=True) + EPS)


def _rope_table_kernel(pos_ref, cos_ref, sin_ref):
    lane = lax.broadcasted_iota(jnp.int32, (1, LANES), 1)
    in_head = lane & (HEAD_DIM - 1)
    freq = (in_head & (ROPE_HALF - 1)).astype(_F32)
    inv = jnp.exp(freq * (-2.0 / ROPE_DIM * math.log(ROPE_THETA)))
    inv = jnp.where(in_head < ROPE_DIM, inv, 0.0)
    ang = pos_ref[...].astype(_F32) * inv
    cos_ref[...] = jnp.cos(ang)
    sin = jnp.sin(ang)
    sin_ref[...] = jnp.where(in_head < ROPE_HALF, -sin, sin)


def _rope_tables(positions):
    n = positions.size
    tm = min(TOKEN_TILE, n)
    pos = positions.reshape(n, 1)
    return pl.pallas_call(
        _rope_table_kernel,
        grid=(n // tm,),
        in_specs=[pl.BlockSpec((tm, 1), lambda i: (i, 0))],
        out_specs=[pl.BlockSpec((tm, LANES), lambda i: (i, 0))] * 2,
        out_shape=[jax.ShapeDtypeStruct((n, LANES), _F32)] * 2,
        compiler_params=_params(1),
        name="rope_tables",
    )(pos)


def _ffn_kernel(h_ref, g_ref, wg_ref, wu_ref, wd_ref, *rest, final):
    if final:
        fg_ref, o_ref = rest
    else:
        (o_ref,) = rest
    x = h_ref[...]
    u = (_rms_scale(x) * g_ref[...]).astype(_BF16)
    gate = _dot(u, wg_ref[...])
    up = _dot(u, wu_ref[...])
    act = (gate * _sigmoid(gate) * up).astype(_BF16)
    y = x + 0.5 * _dot(act, wd_ref[...])
    if final:
        y = _rms_scale(y) * fg_ref[...]
    o_ref[...] = y


def _ffn(h, g, wg, wu, wd, final_g=None):
    n = h.shape[0]
    tm = min(TOKEN_TILE, n)
    final = final_g is not None
    row = pl.BlockSpec((tm, D_MODEL), lambda i: (i, 0))
    vec = _resident((1, D_MODEL), lambda i: (0, 0))
    in_specs = [row, vec,
                _resident((D_MODEL, D_FF), lambda i: (0, 0)),
                _resident((D_MODEL, D_FF), lambda i: (0, 0)),
                _resident((D_FF, D_MODEL), lambda i: (0, 0))]
    args = [h, g.reshape(1, D_MODEL), wg, wu, wd]
    if final:
        in_specs.append(vec)
        args.append(final_g.reshape(1, D_MODEL))
    return pl.pallas_call(
        functools.partial(_ffn_kernel, final=final),
        grid=(n // tm,),
        in_specs=in_specs,
        out_specs=row,
        out_shape=jax.ShapeDtypeStruct((n, D_MODEL), _F32),
        compiler_params=_params(1),
        name="ffn_final" if final else "ffn",
    )(*args)


def _mix_in_kernel(h_ref, g_ref, cos_ref, sin_ref, w_ref, wvt_ref,
                   qd_ref, kd_ref, qn_ref, ks_ref, kw_ref, vt_ref,
                   craw_ref, gn_ref, ga_ref, gb_ref):
    x = h_ref[...]
    u = (_rms_scale(x) * g_ref[...]).astype(_BF16)
    cos = cos_ref[...]
    sin = sin_ref[...]
    lane = lax.broadcasted_iota(jnp.int32, (1, LANES), 1)
    first_half = (lane & (HEAD_DIM - 1)) < ROPE_HALF

    def proj(c0, width):
        return _dot(u, w_ref[:, c0:c0 + width])

    def rope_store(o_ref, c0, width, scale):
        z = proj(c0, width)
        for c in range(width // LANES):
            zc = z[:, c * LANES:(c + 1) * LANES]
            partner = jnp.where(first_half, pltpu.roll(zc, LANES - ROPE_HALF, axis=1),
                                pltpu.roll(zc, ROPE_HALF, axis=1))
            r = zc * cos + partner * sin
            if scale != 1.0:
                r = r * scale
            o_ref[:, c * LANES:(c + 1) * LANES] = r.astype(o_ref.dtype)

    rope_store(qd_ref, _C_QD, 512, ATTN_SCALE)
    rope_store(kd_ref, _C_KD, 512, 1.0)
    rope_store(qn_ref, _C_QN, 512, ATTN_SCALE)
    rope_store(ks_ref, _C_KS, 256, 1.0)
    rope_store(kw_ref, _C_KW, 256, 1.0)
    vt_ref[...] = _dot_nt(wvt_ref[...], u).astype(_BF16)
    craw = proj(_C_CRAW, 256)
    craw_ref[0] = craw[:, :LANES]
    craw_ref[1] = craw[:, LANES:]
    gn_ref[...] = _sigmoid(proj(_C_GN, 256))
    ga_ref[...] = _sigmoid(proj(_C_GA, D_MODEL)).astype(_BF16)
    gb_ref[...] = _sigmoid(proj(_C_GB, D_MODEL)).astype(_BF16)


def _mix_in(h, g, cos_t, sin_t, w, wvt):
    n = h.shape[0]
    tm = min(TOKEN_TILE, n)
    vd_width = wvt.shape[0]

    def row(width):
        return pl.BlockSpec((tm, width), lambda i: (i, 0))

    def bf16_rows(width):
        return row(width), jax.ShapeDtypeStruct((n, width), _BF16)

    outs = [bf16_rows(512), bf16_rows(512), bf16_rows(512), bf16_rows(256), bf16_rows(256),
            (pl.BlockSpec((vd_width, tm), lambda i: (0, i)), jax.ShapeDtypeStruct((vd_width, n), _BF16))]
    out_specs = [spec for spec, _ in outs]
    out_shape = [shape for _, shape in outs]
    out_specs += [pl.BlockSpec((2, tm, LANES), lambda i: (0, i, 0)), row(256), row(D_MODEL), row(D_MODEL)]
    out_shape += [jax.ShapeDtypeStruct((2, n, LANES), _F32),
                  jax.ShapeDtypeStruct((n, 256), _F32),
                  jax.ShapeDtypeStruct((n, D_MODEL), _BF16),
                  jax.ShapeDtypeStruct((n, D_MODEL), _BF16)]
    return pl.pallas_call(
        _mix_in_kernel,
        grid=(n // tm,),
        in_specs=[row(D_MODEL), _resident((1, D_MODEL), lambda i: (0, 0)),
                  row(LANES), row(LANES),
                  _resident((D_MODEL, _C_TOTAL), lambda i: (0, 0)),
                  _resident((vd_width, D_MODEL), lambda i: (0, 0))],
        out_specs=out_specs,
        out_shape=out_shape,
        compiler_params=_params(1),
        name="mix_in",
    )(h, g.reshape(1, D_MODEL), cos_t, sin_t, w, wvt)


def _compress_kernel(x_ref, pos_ref, w1_ref, w2_ref, w2t_ref, o_ref, ot_ref, *, n_cmp):
    x = x_ref[0, 0]
    rows = x.shape[0]
    x_top = (x + pos_ref[0, 0]).astype(_BF16)
    x_bot = (x + pos_ref[0, 1]).astype(_BF16)
    valid = lax.broadcasted_iota(jnp.int32, (rows, 1), 0) < n_cmp
    valid_t = lax.broadcasted_iota(jnp.int32, (1, rows), 1) < n_cmp
    for g in range(NSA_GROUPS):
        top = _dot(x_top, w1_ref[0, g, 0])
        bot = _dot(x_bot, w1_ref[0, g, 1])
        pre = top + pltpu.roll(bot, rows - 1, axis=0)
        hid = (pre * _sigmoid(pre)).astype(_BF16)
        out = _dot(hid, w2_ref[0])
        out = jnp.where(valid, out, 0.0)
        o_ref[0, 0, :, g * LANES:(g + 1) * LANES] = out.astype(o_ref.dtype)
        out_t = jnp.where(valid_t, _dot_nt(w2t_ref[0], hid), 0.0)
        ot_ref[0, 0, g * HEAD_DIM:(g + 1) * HEAD_DIM, :] = out_t.astype(ot_ref.dtype)


def _compress(craw, pos_x, w1_x, w2_x, w2_t, batch, seq):
    rows = seq // CMP_STRIDE
    n_cmp = (seq - CMP_BLOCK) // CMP_STRIDE + 1
    width = CMP_STRIDE * LANES
    x = craw.reshape(2, batch, rows, width)
    return pl.pallas_call(
        functools.partial(_compress_kernel, n_cmp=n_cmp),
        grid=(2, batch),
        in_specs=[pl.BlockSpec((1, 1, rows, width), lambda k, b: (k, b, 0, 0)),
                  pl.BlockSpec((1, 2, 1, width), lambda k, b: (k, 0, 0, 0)),
                  pl.BlockSpec((1, NSA_GROUPS, 2, width, CMP_HIDDEN), lambda k, b: (k, 0, 0, 0, 0)),
                  pl.BlockSpec((1, CMP_HIDDEN, LANES), lambda k, b: (k, 0, 0)),
                  pl.BlockSpec((1, HEAD_DIM, CMP_HIDDEN), lambda k, b: (k, 0, 0))],
        out_specs=[pl.BlockSpec((1, 1, rows, NSA_GROUPS * LANES), lambda k, b: (k, b, 0, 0)),
                   pl.BlockSpec((1, 1, NSA_GROUPS * HEAD_DIM, rows), lambda k, b: (k, b, 0, 0))],
        out_shape=[jax.ShapeDtypeStruct((2, batch, rows, NSA_GROUPS * LANES), _BF16),
                   jax.ShapeDtypeStruct((2, batch, NSA_GROUPS * HEAD_DIM, rows), _BF16)],
        compiler_params=_params(2),
        name="compress",
    )(x, pos_x, w1_x, w2_x, w2_t)


def _mask_bias(mask):
    return jnp.where(mask, 0.0, -jnp.inf).astype(_F32)


def _diff_attn_kernel(lam_ref, q_ref, k_ref, vt_ref, o_ref, *, lam_init):
    i = pl.program_id(1)
    tq = DIFF_TILE
    lane = lax.broadcasted_iota(jnp.int32, (1, LANES), 1)
    heads = range(DIFF_HEADS)

    qs = []
    for h in heads:
        q = q_ref[:, h * LANES:(h + 1) * LANES]
        zero = jnp.zeros_like(q)
        qs.append(jnp.concatenate([jnp.where(lane < HEAD_DIM, q, zero), jnp.where(lane >= HEAD_DIM, q, zero)], axis=0))

    def scores(h, keys):
        return _dot_nt(k_ref[keys, h * LANES:(h + 1) * LANES], qs[h])

    def weighted(h, p, keys):
        ones_rows = jnp.ones((ONES_ROWS, keys.size), _BF16)
        v_one = jnp.concatenate([vt_ref[h * LANES:(h + 1) * LANES, keys], ones_rows], axis=0)
        return _dot(v_one, p)

    def chunk(first_key, n_keys, carry, bias):
        keys = pl.ds(pl.multiple_of(first_key, tq), n_keys)
        s = [scores(h, keys) for h in heads]
        if bias is not None:
            s = [x + bias for x in s]
        m_new = [jnp.max(x, axis=0, keepdims=True) for x in s]
        if carry is not None:
            m_new = [jnp.maximum(carry[h][0], m_new[h]) for h in heads]
        p = [jnp.exp((s[h] - m_new[h]).astype(_BF16)) for h in heads]
        res = [weighted(h, p[h], keys) for h in heads]
        if carry is not None:
            res = [jnp.exp(carry[h][0] - m_new[h]) * carry[h][1] + res[h] for h in heads]
        return tuple((m_new[h], res[h]) for h in heads)

    key = lax.broadcasted_iota(jnp.int32, (tq, tq), 0)
    qry = lax.broadcasted_iota(jnp.int32, (tq, tq), 1)
    tri = _mask_bias(key <= qry)
    carry = chunk(i * tq, tq, None, jnp.concatenate([tri, tri], axis=1))
    carry = lax.fori_loop(0, i // 2, lambda j, c: chunk(j * (2 * tq), 2 * tq, c, None), carry)
    carry = lax.cond(i % 2 == 1, lambda c: chunk((i - 1) * tq, tq, c, None), lambda c: c, carry)
    lp = lam_ref[...]
    lam = (jnp.exp(jnp.sum(lp[0:1] * lp[1:2], axis=1, keepdims=True))
           - jnp.exp(jnp.sum(lp[2:3] * lp[3:4], axis=1, keepdims=True)) + lam_init)
    for h in heads:
        _, acc = carry[h]
        o = acc[:LANES] * (1.0 / acc[LANES:LANES + 1])
        o = o[:, :tq] - lam * o[:, tq:]
        o = o * lax.rsqrt(jnp.mean(o * o, axis=0, keepdims=True) + EPS) * (1.0 - lam_init)
        o_ref[:, h * LANES:(h + 1) * LANES] = o.T.astype(o_ref.dtype)


def _diff_attn(lam_p, qd, kd, vdt, batch, seq, lam_init):
    tq = DIFF_TILE
    nq = seq // tq
    n = batch * seq
    width = DIFF_HEADS * LANES
    return pl.pallas_call(
        functools.partial(_diff_attn_kernel, lam_init=lam_init),
        grid=(batch, nq),
        in_specs=[pl.BlockSpec((4, HEAD_DIM), lambda b, i: (0, 0)),
                  pl.BlockSpec((tq, width), lambda b, i: (b * nq + i, 0)),
                  pl.BlockSpec((seq, width), lambda b, i: (b, 0)),
                  pl.BlockSpec((width, seq), lambda b, i: (0, b))],
        out_specs=pl.BlockSpec((tq, width), lambda b, i: (b * nq + i, 0)),
        out_shape=jax.ShapeDtypeStruct((n, width), _BF16),
        compiler_params=_params(2),
        name="diff_attn",
    )(lam_p, qd, kd, vdt)


def _nsa_kernel(q_ref, gate_ref, ovt_ref, blk_ref, kc_ref, vct_ref, ks_ref, kw_ref, vt_ref, o_ref):
    i = pl.program_id(1)
    tq = NSA_TILE
    hpg = NSA_HPG
    groups = range(NSA_GROUPS)
    lane = lax.broadcasted_iota(jnp.int32, (1, LANES), 1)
    low = lane < HEAD_DIM
    q0 = i * tq

    def per_head(x):
        return jnp.concatenate([x] * hpg, axis=0)

    def group_cols(ref, rows, g):
        return ref[rows, g * LANES:(g + 1) * LANES]

    def stack_q(g):
        parts = []
        for j in range(hpg):
            c0 = g * hpg * HEAD_DIM + (j // 2) * LANES
            qc = q_ref[:, c0:c0 + LANES]
            keep = low if j % 2 == 0 else jnp.logical_not(low)
            parts.append(jnp.where(keep, qc, jnp.zeros_like(qc)))
        return jnp.concatenate(parts, axis=0)

    def per_head_cols(x):
        return jnp.concatenate([x] * hpg, axis=1)

    qpos_row = q0 + lax.broadcasted_iota(jnp.int32, (1, tq), 1)

    def compressed_and_select(g, qs):
        kc = kc_ref[0, 0, :, g * LANES:(g + 1) * LANES]
        vc_t = vct_ref[0, 0, g * HEAD_DIM:(g + 1) * HEAD_DIM, :]
        ncp = kc.shape[0]
        s_c = _dot_nt(kc, qs)
        cmp_end = lax.broadcasted_iota(jnp.int32, (ncp, 1), 0) * CMP_STRIDE + (CMP_BLOCK - 1)
        s_c = s_c + per_head_cols(_mask_bias(cmp_end <= qpos_row))
        mx = jnp.max(s_c, axis=0, keepdims=True)
        mx = jnp.where(mx == -jnp.inf, 0.0, mx)
        p_c = jnp.exp(s_c - mx)
        p_c = p_c * (1.0 / jnp.maximum(jnp.sum(p_c, axis=0, keepdims=True), 1e-30))
        o_c = _dot(vc_t, p_c.astype(_BF16))

        p_sum = p_c[:, 0:tq]
        for j in range(1, hpg):
            p_sum = p_sum + p_c[:, j * tq:(j + 1) * tq]
        ov_t = ovt_ref[...]
        p_hi = p_sum.astype(_BF16)
        r1 = p_sum - p_hi.astype(_F32)
        p_mid = r1.astype(_BF16)
        p_lo = (r1 - p_mid.astype(_F32)).astype(_BF16)
        imp = _dot(ov_t, p_hi) + _dot(ov_t, p_mid) + _dot(ov_t, p_lo)

        jsel = lax.broadcasted_iota(jnp.int32, (SEL_PAD, 1), 0)
        cur = qpos_row >> SEL_SHIFT
        forced = (jsel == 0) | (jsel == cur) | (jsel == cur - 1)
        future = jsel * SEL_BLOCK > qpos_row
        work = jnp.where(future, -1.0, jnp.where(forced, FORCED_SCORE, imp))

        jsel_f = jsel.astype(_F32)
        chosen = jnp.zeros((SEL_PAD, tq), _F32)
        for _ in range(SEL_TOPK):
            best = jnp.max(work, axis=0, keepdims=True)
            first = jnp.min(jnp.where(work == best, jsel_f, float(SEL_PAD)), axis=0, keepdims=True)
            pick = jsel_f == first
            chosen = jnp.where(pick, 1.0, chosen)
            work = jnp.where(pick, -jnp.inf, work)
        return o_c, chosen

    qs, o_c, q_sel = [], [], []
    for g in groups:
        qs.append(stack_q(g))
        oc, chosen = compressed_and_select(g, qs[g])
        o_c.append(oc)
        penalty = ((chosen - 1.0) * MASKED_SCORE).T.astype(_BF16)
        q_sel.append(jnp.concatenate([qs[g], per_head(penalty)], axis=1))


    def v_rows(kind, g, keys):
        r0 = (kind * NSA_GROUPS + g) * HEAD_DIM
        ones_rows = jnp.ones((ONES_ROWS, keys.size), _BF16)
        return jnp.concatenate([vt_ref[r0:r0 + HEAD_DIM, keys], ones_rows], axis=0)

    def sel_scores(g, keys):
        k_blk = jnp.concatenate([group_cols(ks_ref, keys, g), blk_ref[keys, :]], axis=1)
        return _dot_nt(k_blk, q_sel[g])

    def sel_chunk(first_key, n_keys, carry, bias):
        keys = pl.ds(pl.multiple_of(first_key, tq), n_keys)
        s = [sel_scores(g, keys) for g in groups]
        if bias is not None:
            s = [x + bias for x in s]
        m_new = [jnp.max(x, axis=0, keepdims=True) for x in s]
        if carry is not None:
            m_new = [jnp.maximum(carry[g][0], m_new[g]) for g in groups]
        p = [jnp.exp((s[g] - m_new[g]).astype(_BF16)) for g in groups]
        res = [_dot(v_rows(0, g, keys), p[g]) for g in groups]
        if carry is not None:
            res = [jnp.exp(carry[g][0] - m_new[g]) * carry[g][1] + res[g] for g in groups]
        return tuple((m_new[g], res[g]) for g in groups)

    wlen = WINDOW + tq
    w0 = jnp.maximum(q0 - WINDOW, 0)
    wkeys = pl.ds(pl.multiple_of(w0, tq), wlen)
    key_w = lax.broadcasted_iota(jnp.int32, (wlen, tq), 0)
    qry_w = lax.broadcasted_iota(jnp.int32, (wlen, tq), 1) + (q0 - w0)
    band = per_head_cols(
        jnp.where(key_w <= qry_w, jnp.where(key_w > qry_w - WINDOW, 0.0, -jnp.inf), -jnp.inf).astype(_F32))
    s_w = [_dot_nt(group_cols(kw_ref, wkeys, g), qs[g]) + band for g in groups]
    p_w = [jnp.exp((x - jnp.max(x, axis=0, keepdims=True)).astype(_BF16)) for x in s_w]
    res_w = [_dot(v_rows(1, g, wkeys), p_w[g]) for g in groups]

    key_t = lax.broadcasted_iota(jnp.int32, (tq, tq), 0)
    qry_t = lax.broadcasted_iota(jnp.int32, (tq, tq), 1)
    carry = sel_chunk(q0, tq, None, per_head_cols(_mask_bias(key_t <= qry_t)))
    carry = lax.fori_loop(0, i // 2, lambda j, c: sel_chunk(j * (2 * tq), 2 * tq, c, None), carry)
    carry = lax.cond(i % 2 == 1, lambda c: sel_chunk(q0 - tq, tq, c, None), lambda c: c, carry)

    def normalised_rows(res):
        return res[:HEAD_DIM] * (1.0 / res[HEAD_DIM:HEAD_DIM + 1])

    for g in groups:
        gate_t = gate_ref[:, g * LANES:(g + 1) * LANES].T
        o_s = normalised_rows(carry[g][1])
        o_w = normalised_rows(res_w[g])
        mixed = []
        for j in range(hpg):
            cols = slice(j * tq, (j + 1) * tq)
            mixed.append(gate_t[3 * j:3 * j + 1] * o_c[g][:, cols] + gate_t[3 * j + 1:3 * j + 2] * o_s[:, cols]
                         + gate_t[3 * j + 2:3 * j + 3] * o_w[:, cols])
        for c in range(hpg // 2):
            pair = jnp.concatenate([mixed[2 * c], mixed[2 * c + 1]], axis=0)
            col = (g * (hpg // 2) + c) * LANES
            o_ref[:, col:col + LANES] = pair.T.astype(o_ref.dtype)


def _overlap_matrix(seq):
    rows = seq // CMP_STRIDE
    n_cmp = (seq - CMP_BLOCK) // CMP_STRIDE + 1
    n_sel = seq // SEL_BLOCK
    cs = np.arange(rows)[:, None] * CMP_STRIDE
    ss = np.arange(SEL_PAD)[None, :] * SEL_BLOCK
    ov = np.clip(np.minimum(cs + CMP_BLOCK, ss + SEL_BLOCK) - np.maximum(cs, ss), 0, None) / CMP_BLOCK
    ov = ov * (np.arange(rows)[:, None] < n_cmp) * (np.arange(SEL_PAD)[None, :] < n_sel)
    return jnp.asarray(ov.T, dtype=_BF16)


def _block_onehot(seq):
    hit = (np.arange(seq)[:, None] >> SEL_SHIFT) == np.arange(SEL_PAD)[None, :]
    return jnp.asarray(hit, dtype=_BF16)


def _nsa(qn, gn, ckv, ckv_t, ks, kw, vt, batch, seq):
    tq = NSA_TILE
    nq = seq // tq
    n = batch * seq
    rows = seq // CMP_STRIDE
    width = NSA_GROUPS * LANES
    assert seq // SEL_BLOCK <= SEL_PAD and WINDOW % tq == 0 and seq >= WINDOW + tq
    assert _VT_DIFF_ROWS % _VT_NSA_ROWS == 0
    qmap = lambda b, i: (b * nq + i, 0)
    kvmap = lambda b, i: (b, 0)
    return pl.pallas_call(
        _nsa_kernel,
        grid=(batch, nq),
        in_specs=[pl.BlockSpec((tq, NSA_HEADS * HEAD_DIM), qmap),
                  pl.BlockSpec((tq, width), qmap),
                  _resident((SEL_PAD, rows), lambda b, i: (0, 0)),
                  _resident((seq, SEL_PAD), lambda b, i: (0, 0)),
                  pl.BlockSpec((1, 1, rows, width), lambda b, i: (0, b, 0, 0)),
                  pl.BlockSpec((1, 1, NSA_GROUPS * HEAD_DIM, rows), lambda b, i: (1, b, 0, 0)),
                  pl.BlockSpec((seq, width), kvmap), pl.BlockSpec((seq, width), kvmap),
                  pl.BlockSpec((_VT_NSA_ROWS, seq), lambda b, i: (_VT_DIFF_ROWS // _VT_NSA_ROWS, b))],
        out_specs=pl.BlockSpec((tq, NSA_HEADS * HEAD_DIM), qmap),
        out_shape=jax.ShapeDtypeStruct((n, NSA_HEADS * HEAD_DIM), _BF16),
        compiler_params=_params(2),
        name="nsa",
    )(qn, gn, _overlap_matrix(seq), _block_onehot(seq), ckv, ckv_t, ks, kw, vt)


def _merge_kernel(h_ref, oa_ref, ob_ref, ga_ref, gb_ref, wa_ref, wb_ref, wo_ref, o_ref):
    y = (ga_ref[...].astype(_F32) * _dot(oa_ref[...], wa_ref[...])
         + gb_ref[...].astype(_F32) * _dot(ob_ref[...], wb_ref[...]))
    o_ref[...] = h_ref[...] + _dot(y.astype(_BF16), wo_ref[...])


def _merge(h, oa, ob, ga, gb, wa, wb, wo):
    n = h.shape[0]
    tm = min(TOKEN_TILE, n)

    def row(width):
        return pl.BlockSpec((tm, width), lambda i: (i, 0))

    return pl.pallas_call(
        _merge_kernel,
        grid=(n // tm,),
        in_specs=[row(D_MODEL), row(512), row(512), row(D_MODEL), row(D_MODEL),
                  _resident((512, D_MODEL), lambda i: (0, 0)),
                  _resident((512, D_MODEL), lambda i: (0, 0)),
                  _resident((D_MODEL, D_MODEL), lambda i: (0, 0))],
        out_specs=row(D_MODEL),
        out_shape=jax.ShapeDtypeStruct((n, D_MODEL), _F32),
        compiler_params=_params(1),
        name="merge",
    )(h, oa, ob, ga, gb, wa, wb, wo)


def _dup_groups(w):
    a, b = w[:, :HEAD_DIM], w[:, HEAD_DIM:]
    return jnp.concatenate([a, a, b, b], axis=1)


def _prep_w_in(w):
    col = lambda k: w[:, _OFFS[k]:_OFFS[k + 1]]
    q_d, k_d, v_d, q_n, kc, vc, ks, vs, kw, vw, g_n, g_a, g_b = [col(k) for k in range(13)]
    per_group = NSA_HPG * 3
    pad = jnp.zeros((w.shape[0], LANES - per_group), w.dtype)
    gn = jnp.concatenate([g_n[:, :per_group], pad, g_n[:, per_group:], pad], axis=1)
    out = jnp.concatenate([q_d, k_d, q_n, _dup_groups(ks), _dup_groups(kw), kc, vc, gn, g_a, g_b], axis=1)
    assert out.shape[1] == _C_TOTAL
    w_vt = jnp.concatenate([v_d, vs, vw], axis=1).T
    assert w_vt.shape[0] == _VT_DIFF_ROWS + _VT_NSA_ROWS
    return out.astype(_BF16), w_vt.astype(_BF16)


def _prep_compress(pos, w1, w2):
    half = CMP_BLOCK // 2
    w1 = w1.reshape(2, 2, half, HEAD_DIM, CMP_HIDDEN)
    zeros = jnp.zeros_like(w1)
    per_group = []
    for g in range(NSA_GROUPS):
        lanes = [zeros, zeros]
        lanes[g] = w1
        per_group.append(jnp.concatenate(lanes, axis=3))
    w1_x = jnp.stack(per_group, axis=1).reshape(2, NSA_GROUPS, 2, half * LANES, CMP_HIDDEN).astype(_BF16)
    pos = pos.reshape(2, 2, half, HEAD_DIM)
    pos_x = jnp.concatenate([pos, pos], axis=3).reshape(2, 2, 1, half * LANES).astype(_F32)
    w2_x = jnp.concatenate([w2, w2], axis=2).astype(_BF16)
    w2_t = jnp.swapaxes(w2, 1, 2).astype(_BF16)
    return pos_x, w1_x, w2_x, w2_t


def kernel(x, positions, ffn1_norm, ffn1_w_gu, ffn1_w_down, mix_norm, w_in, diff_lambda, cmp_pos, cmp_w1, cmp_w2, w_branch_a, w_branch_b, w_out, ffn2_norm, ffn2_w_gu, ffn2_w_down, final_norm):
    batch, seq, _ = x.shape
    n = batch * seq
    depth = w_in.shape[0]
    assert seq % max(NSA_TILE, DIFF_TILE, CMP_STRIDE * 8) == 0
    cos_t, sin_t = _rope_tables(positions)
    h = x.reshape(n, D_MODEL)
    for l in range(depth):
        bf = lambda a: a.astype(_BF16)
        h = _ffn(h, ffn1_norm[l], bf(ffn1_w_gu[l][:, :D_FF]), bf(ffn1_w_gu[l][:, D_FF:]), bf(ffn1_w_down[l]))
        (qd, kd, qn, ks, kw, vt, craw, gn, ga, gb) = _mix_in(h, mix_norm[l], cos_t, sin_t, *_prep_w_in(w_in[l]))
        ckv, ckv_t = _compress(craw, *_prep_compress(cmp_pos[l], cmp_w1[l], cmp_w2[l]), batch, seq)
        lam_init = 0.8 - 0.6 * math.exp(-0.3 * l)
        o_a = _diff_attn(diff_lambda[l].astype(_F32), qd, kd, vt, batch, seq, lam_init)
        o_b = _nsa(qn, gn, ckv, ckv_t, ks, kw, vt, batch, seq)
        h = _merge(h, o_a, o_b, ga, gb, bf(w_branch_a[l]), bf(w_branch_b[l]), bf(w_out[l]))
        last = l == depth - 1
        h = _ffn(h, ffn2_norm[l], bf(ffn2_w_gu[l][:, :D_FF]), bf(ffn2_w_gu[l][:, D_FF:]), bf(ffn2_w_down[l]),
                 final_g=final_norm if last else None)
    return h.reshape(batch, seq, D_MODEL)
```

```python
import functools
import math

import numpy as np
import jax
import jax.numpy as jnp
from jax import lax
from jax.experimental import pallas as pl
from jax.experimental.pallas import tpu as pltpu

D_MODEL = 1024
D_FF = 2816
HEAD_DIM = 64
ROPE_DIM = HEAD_DIM // 4
ROPE_HALF = ROPE_DIM // 2
ROPE_THETA = 500000.0
EPS = 1e-6
DIFF_HEADS = 4
NSA_HEADS = 8
NSA_GROUPS = 2
NSA_HPG = NSA_HEADS // NSA_GROUPS
CMP_BLOCK = 32
CMP_STRIDE = 16
CMP_HIDDEN = 256
SEL_BLOCK = 64
SEL_SHIFT = SEL_BLOCK.bit_length() - 1
SEL_TOPK = 8
WINDOW = 512
FORCED_SCORE = 1.0e4
MASKED_SCORE = 2.0 ** 100
ATTN_SCALE = HEAD_DIM ** -0.5

_SPLITS = (512, 512, 512, 512, 128, 128, 128, 128, 128, 128, 24, 1024, 1024)
_OFFS = np.concatenate([[0], np.cumsum(_SPLITS)]).tolist()

LANES = 128
VMEM_LIMIT_BYTES = 56 * 1024 * 1024
TOKEN_TILE = 512
DIFF_TILE = 256
NSA_TILE = 256
SEL_PAD = 128

_C_QD, _C_KD, _C_QN, _C_KS, _C_KW = 0, 512, 1024, 1536, 1792
_C_ROPE_END = 2048
_C_CRAW, _C_GN, _C_GA, _C_GB = 2048, 2304, 2560, 3584
_C_TOTAL = 4608
_VT_DIFF_ROWS = 512
_VT_NSA_ROWS = 2 * NSA_GROUPS * HEAD_DIM
BF16_SUBLANES = 16
ONES_ROWS = BF16_SUBLANES

_BF16 = jnp.bfloat16
_F32 = jnp.float32


def _params(n_grid):
    return pltpu.CompilerParams(
        dimension_semantics=("arbitrary",) * n_grid,
        vmem_limit_bytes=VMEM_LIMIT_BYTES,
    )


def _resident(shape, index_map):
    return pl.BlockSpec(shape, index_map, pipeline_mode=pl.Buffered(1))


def _dot(a, b):
    return jnp.dot(a, b, preferred_element_type=_F32)


def _dot_nt(a, b):
    return lax.dot_general(a, b, (((1,), (1,)), ((), ())), preferred_element_type=_F32)


def _sigmoid(x):
    return 1.0 / (1.0 + jnp.exp(-x))


def _rms_scale(x):
    return x * lax.rsqrt(jnp.mean(x * x, axis=-1, keepdims=True) + EPS)


def _rope_table_kernel(pos_ref, cos_ref, sin_ref):
    lane = lax.broadcasted_iota(jnp.int32, (1, LANES), 1)
    in_head = lane & (HEAD_DIM - 1)
    freq = (in_head & (ROPE_HALF - 1)).astype(_F32)
    inv = jnp.exp(freq * (-2.0 / ROPE_DIM * math.log(ROPE_THETA)))
    inv = jnp.where(in_head < ROPE_DIM, inv, 0.0)
    ang = pos_ref[...].astype(_F32) * inv
    cos_ref[...] = jnp.cos(ang)
    sin = jnp.sin(ang)
    sin_ref[...] = jnp.where(in_head < ROPE_HALF, -sin, sin)


def _rope_tables(positions):
    n = positions.size
    tm = min(TOKEN_TILE, n)
    pos = positions.reshape(n, 1)
    return pl.pallas_call(
        _rope_table_kernel,
        grid=(n // tm,),
        in_specs=[pl.BlockSpec((tm, 1), lambda i: (i, 0))],
        out_specs=[pl.BlockSpec((tm, LANES), lambda i: (i, 0))] * 2,
        out_shape=[jax.ShapeDtypeStruct((n, LANES), _F32)] * 2,
        compiler_params=_params(1),
        name="rope_tables",
    )(pos)


def _ffn_kernel(h_ref, g_ref, wg_ref, wu_ref, wd_ref, *rest, final):
    if final:
        fg_ref, o_ref = rest
    else:
        (o_ref,) = rest
    x = h_ref[...]
    u = (_rms_scale(x) * g_ref[...]).astype(_BF16)
    gate = _dot(u, wg_ref[...])
    up = _dot(u, wu_ref[...])
    act = (gate * _sigmoid(gate) * up).astype(_BF16)
    y = x + 0.5 * _dot(act, wd_ref[...])
    if final:
        y = _rms_scale(y) * fg_ref[...]
    o_ref[...] = y


def _ffn(h, g, w_gu, wd, final_g=None):
    n = h.shape[0]
    tm = min(TOKEN_TILE, n)
    final = final_g is not None
    row = pl.BlockSpec((tm, D_MODEL), lambda i: (i, 0))
    vec = _resident((1, D_MODEL), lambda i: (0, 0))
    in_specs = [row, vec,
                _resident((D_MODEL, D_FF), lambda i: (0, 0)),
                _resident((D_MODEL, D_FF), lambda i: (0, 1)),
                _resident((D_FF, D_MODEL), lambda i: (0, 0))]
    args = [h, g.reshape(1, D_MODEL), w_gu, w_gu, wd]
    if final:
        in_specs.append(vec)
        args.append(final_g.reshape(1, D_MODEL))
    return pl.pallas_call(
        functools.partial(_ffn_kernel, final=final),
        grid=(n // tm,),
        in_specs=in_specs,
        out_specs=row,
        out_shape=jax.ShapeDtypeStruct((n, D_MODEL), _F32),
        compiler_params=_params(1),
        name="ffn_final" if final else "ffn",
    )(*args)


def _mix_in_kernel(h_ref, g_ref, cos_ref, sin_ref, w_ref, wvt_ref,
                   qd_ref, kd_ref, qn_ref, ks_ref, kw_ref, vt_ref,
                   craw_ref, gn_ref, ga_ref, gb_ref, craw_scr):
    x = h_ref[...]
    u = (_rms_scale(x) * g_ref[...]).astype(_BF16)
    cos = cos_ref[...]
    sin = sin_ref[...]
    lane = lax.broadcasted_iota(jnp.int32, (1, LANES), 1)
    first_half = (lane & (HEAD_DIM - 1)) < ROPE_HALF

    def proj(c0, width):
        return _dot(u, w_ref[:, c0:c0 + width])

    def rope_store(o_ref, c0, width, scale):
        z = proj(c0, width)
        for c in range(width // LANES):
            zc = z[:, c * LANES:(c + 1) * LANES]
            partner = jnp.where(first_half, pltpu.roll(zc, LANES - ROPE_HALF, axis=1),
                                pltpu.roll(zc, ROPE_HALF, axis=1))
            r = zc * cos + partner * sin
            if scale != 1.0:
                r = r * scale
            o_ref[:, c * LANES:(c + 1) * LANES] = r.astype(o_ref.dtype)

    rope_store(qd_ref, _C_QD, 512, ATTN_SCALE)
    rope_store(kd_ref, _C_KD, 512, 1.0)
    rope_store(qn_ref, _C_QN, 512, ATTN_SCALE)
    rope_store(ks_ref, _C_KS, 256, 1.0)
    rope_store(kw_ref, _C_KW, 256, 1.0)
    vt_ref[...] = _dot_nt(wvt_ref[...], u).astype(_BF16)
    craw = proj(_C_CRAW, 256)
    for kind in range(2):
        craw_scr[kind] = craw[:, kind * LANES:(kind + 1) * LANES]
        for t in range(CMP_STRIDE):
            every = pl.ds(t, craw_scr.shape[1] // CMP_STRIDE, stride=CMP_STRIDE)
            craw_ref[kind, :, t * LANES:(t + 1) * LANES] = craw_scr[kind, every, :]
    gn_ref[...] = _sigmoid(proj(_C_GN, 256))
    ga_ref[...] = _sigmoid(proj(_C_GA, D_MODEL)).astype(_BF16)
    gb_ref[...] = _sigmoid(proj(_C_GB, D_MODEL)).astype(_BF16)


def _mix_in(h, g, cos_t, sin_t, w, wvt):
    n = h.shape[0]
    tm = min(TOKEN_TILE, n)
    vd_width = wvt.shape[0]

    def row(width):
        return pl.BlockSpec((tm, width), lambda i: (i, 0))

    def bf16_rows(width):
        return row(width), jax.ShapeDtypeStruct((n, width), _BF16)

    outs = [bf16_rows(512), bf16_rows(512), bf16_rows(512), bf16_rows(256), bf16_rows(256),
            (pl.BlockSpec((vd_width, tm), lambda i: (0, i)), jax.ShapeDtypeStruct((vd_width, n), _BF16))]
    out_specs = [spec for spec, _ in outs]
    out_shape = [shape for _, shape in outs]
    assert tm % (CMP_STRIDE * 8) == 0
    out_specs += [pl.BlockSpec((2, tm // CMP_STRIDE, CMP_STRIDE * LANES), lambda i: (0, i, 0)),
                  row(256), row(D_MODEL), row(D_MODEL)]
    out_shape += [jax.ShapeDtypeStruct((2, n // CMP_STRIDE, CMP_STRIDE * LANES), _F32),
                  jax.ShapeDtypeStruct((n, 256), _F32),
                  jax.ShapeDtypeStruct((n, D_MODEL), _BF16),
                  jax.ShapeDtypeStruct((n, D_MODEL), _BF16)]
    return pl.pallas_call(
        _mix_in_kernel,
        grid=(n // tm,),
        in_specs=[row(D_MODEL), _resident((1, D_MODEL), lambda i: (0, 0)),
                  row(LANES), row(LANES),
                  _resident((D_MODEL, _C_TOTAL), lambda i: (0, 0)),
                  _resident((vd_width, D_MODEL), lambda i: (0, 0))],
        out_specs=out_specs,
        out_shape=out_shape,
        scratch_shapes=[pltpu.VMEM((2, tm, LANES), _F32)],
        compiler_params=_params(1),
        name="mix_in",
    )(h, g.reshape(1, D_MODEL), cos_t, sin_t, w, wvt)


def _compress_kernel(x_ref, pos_ref, w1_ref, w2_ref, w2t_ref, o_ref, ot_ref, *, n_cmp):
    x = x_ref[0, 0]
    rows = x.shape[0]
    x_top = (x + pos_ref[0, 0]).astype(_BF16)
    x_bot = (x + pos_ref[0, 1]).astype(_BF16)
    valid = lax.broadcasted_iota(jnp.int32, (rows, 1), 0) < n_cmp
    valid_t = lax.broadcasted_iota(jnp.int32, (1, rows), 1) < n_cmp
    for g in range(NSA_GROUPS):
        top = _dot(x_top, w1_ref[0, g, 0])
        bot = _dot(x_bot, w1_ref[0, g, 1])
        pre = top + pltpu.roll(bot, rows - 1, axis=0)
        hid = (pre * _sigmoid(pre)).astype(_BF16)
        out = _dot(hid, w2_ref[0])
        out = jnp.where(valid, out, 0.0)
        o_ref[0, 0, :, g * LANES:(g + 1) * LANES] = out.astype(o_ref.dtype)
        out_t = jnp.where(valid_t, _dot_nt(w2t_ref[0], hid), 0.0)
        ot_ref[0, 0, g * HEAD_DIM:(g + 1) * HEAD_DIM, :] = out_t.astype(ot_ref.dtype)


def _compress(craw, pos_x, w1_x, w2_x, w2_t, batch, seq):
    rows = seq // CMP_STRIDE
    n_cmp = (seq - CMP_BLOCK) // CMP_STRIDE + 1
    width = CMP_STRIDE * LANES
    x = craw.reshape(2, batch, rows, width)
    return pl.pallas_call(
        functools.partial(_compress_kernel, n_cmp=n_cmp),
        grid=(2, batch),
        in_specs=[pl.BlockSpec((1, 1, rows, width), lambda k, b: (k, b, 0, 0)),
                  pl.BlockSpec((1, 2, 1, width), lambda k, b: (k, 0, 0, 0)),
                  pl.BlockSpec((1, NSA_GROUPS, 2, width, CMP_HIDDEN), lambda k, b: (k, 0, 0, 0, 0)),
                  pl.BlockSpec((1, CMP_HIDDEN, LANES), lambda k, b: (k, 0, 0)),
                  pl.BlockSpec((1, HEAD_DIM, CMP_HIDDEN), lambda k, b: (k, 0, 0))],
        out_specs=[pl.BlockSpec((1, 1, rows, NSA_GROUPS * LANES), lambda k, b: (k, b, 0, 0)),
                   pl.BlockSpec((1, 1, NSA_GROUPS * HEAD_DIM, rows), lambda k, b: (k, b, 0, 0))],
        out_shape=[jax.ShapeDtypeStruct((2, batch, rows, NSA_GROUPS * LANES), _BF16),
                   jax.ShapeDtypeStruct((2, batch, NSA_GROUPS * HEAD_DIM, rows), _BF16)],
        compiler_params=_params(2),
        name="compress",
    )(x, pos_x, w1_x, w2_x, w2_t)


def _mask_bias(mask):
    return jnp.where(mask, 0.0, -jnp.inf).astype(_F32)


def _diff_attn_kernel(lam_ref, q_ref, k_ref, vt_ref, o_ref, *, lam_init):
    i = pl.program_id(1)
    tq = DIFF_TILE
    lane = lax.broadcasted_iota(jnp.int32, (1, LANES), 1)
    heads = range(DIFF_HEADS)

    qs = []
    for h in heads:
        q = q_ref[:, h * LANES:(h + 1) * LANES]
        zero = jnp.zeros_like(q)
        qs.append(jnp.concatenate([jnp.where(lane < HEAD_DIM, q, zero), jnp.where(lane >= HEAD_DIM, q, zero)], axis=0))

    def scores(h, keys):
        return _dot_nt(k_ref[keys, h * LANES:(h + 1) * LANES], qs[h])

    def weighted(h, p, keys):
        ones_rows = jnp.ones((ONES_ROWS, keys.size), _BF16)
        v_one = jnp.concatenate([vt_ref[h * LANES:(h + 1) * LANES, keys], ones_rows], axis=0)
        return _dot(v_one, p)

    def chunk(first_key, n_keys, carry, bias):
        keys = pl.ds(pl.multiple_of(first_key, tq), n_keys)
        s = [scores(h, keys) for h in heads]
        if bias is not None:
            s = [x + bias for x in s]
        m_new = [jnp.max(x, axis=0, keepdims=True) for x in s]
        if carry is not None:
            m_new = [jnp.maximum(carry[h][0], m_new[h]) for h in heads]
        p = [jnp.exp((s[h] - m_new[h]).astype(_BF16)) for h in heads]
        res = [weighted(h, p[h], keys) for h in heads]
        if carry is not None:
            res = [jnp.exp(carry[h][0] - m_new[h]) * carry[h][1] + res[h] for h in heads]
        return tuple((m_new[h], res[h]) for h in heads)

    key = lax.broadcasted_iota(jnp.int32, (tq, tq), 0)
    qry = lax.broadcasted_iota(jnp.int32, (tq, tq), 1)
    tri = _mask_bias(key <= qry)
    tri = jnp.concatenate([tri, tri], axis=1)
    carry = lax.cond(i % 2 == 1,
                     lambda: chunk((i - 1) * tq, 2 * tq, None, jnp.concatenate([jnp.zeros_like(tri), tri], axis=0)),
                     lambda: chunk(i * tq, tq, None, tri))
    carry = lax.fori_loop(0, i // 2, lambda j, c: chunk(j * (2 * tq), 2 * tq, c, None), carry)
    lp = lam_ref[...]
    lam = (jnp.exp(jnp.sum(lp[0:1] * lp[1:2], axis=1, keepdims=True))
           - jnp.exp(jnp.sum(lp[2:3] * lp[3:4], axis=1, keepdims=True)) + lam_init)
    for h in heads:
        _, acc = carry[h]
        o = acc[:LANES] * (1.0 / acc[LANES:LANES + 1])
        o = o[:, :tq] - lam * o[:, tq:]
        o = o * lax.rsqrt(jnp.mean(o * o, axis=0, keepdims=True) + EPS) * (1.0 - lam_init)
        o_ref[:, h * LANES:(h + 1) * LANES] = o.T.astype(o_ref.dtype)


def _diff_attn(lam_p, qd, kd, vdt, batch, seq, lam_init):
    tq = DIFF_TILE
    nq = seq // tq
    n = batch * seq
    width = DIFF_HEADS * LANES
    return pl.pallas_call(
        functools.partial(_diff_attn_kernel, lam_init=lam_init),
        grid=(batch, nq),
        in_specs=[pl.BlockSpec((4, HEAD_DIM), lambda b, i: (0, 0)),
                  pl.BlockSpec((tq, width), lambda b, i: (b * nq + i, 0)),
                  pl.BlockSpec((seq, width), lambda b, i: (b, 0)),
                  pl.BlockSpec((width, seq), lambda b, i: (0, b))],
        out_specs=pl.BlockSpec((tq, width), lambda b, i: (b * nq + i, 0)),
        out_shape=jax.ShapeDtypeStruct((n, width), _BF16),
        compiler_params=_params(2),
        name="diff_attn",
    )(lam_p, qd, kd, vdt)


def _nsa_kernel(q_ref, gate_ref, ovt_ref, blk_ref, kc_ref, vct_ref, ks_ref, kw_ref, vt_ref, o_ref):
    i = pl.program_id(1)
    tq = NSA_TILE
    hpg = NSA_HPG
    groups = range(NSA_GROUPS)
    lane = lax.broadcasted_iota(jnp.int32, (1, LANES), 1)
    low = lane < HEAD_DIM
    q0 = i * tq

    def per_head(x):
        return jnp.concatenate([x] * hpg, axis=0)

    def group_cols(ref, rows, g):
        return ref[rows, g * LANES:(g + 1) * LANES]

    def stack_q(g):
        parts = []
        for j in range(hpg):
            c0 = g * hpg * HEAD_DIM + (j // 2) * LANES
            qc = q_ref[:, c0:c0 + LANES]
            keep = low if j % 2 == 0 else jnp.logical_not(low)
            parts.append(jnp.where(keep, qc, jnp.zeros_like(qc)))
        return jnp.concatenate(parts, axis=0)

    def per_head_cols(x):
        return jnp.concatenate([x] * hpg, axis=1)

    qpos_row = q0 + lax.broadcasted_iota(jnp.int32, (1, tq), 1)

    def compressed_and_select(g, qs):
        kc = kc_ref[0, 0, :, g * LANES:(g + 1) * LANES]
        vc_t = vct_ref[0, 0, g * HEAD_DIM:(g + 1) * HEAD_DIM, :]
        ncp = kc.shape[0]
        s_c = _dot_nt(kc, qs)
        cmp_end = lax.broadcasted_iota(jnp.int32, (ncp, 1), 0) * CMP_STRIDE + (CMP_BLOCK - 1)
        s_c = s_c + per_head_cols(_mask_bias(cmp_end <= qpos_row))
        mx = jnp.max(s_c, axis=0, keepdims=True)
        mx = jnp.where(mx == -jnp.inf, 0.0, mx)
        p_c = jnp.exp(s_c - mx)
        p_c = p_c * (1.0 / jnp.maximum(jnp.sum(p_c, axis=0, keepdims=True), 1e-30))
        o_c = _dot(vc_t, p_c.astype(_BF16))

        p_sum = p_c[:, 0:tq]
        for j in range(1, hpg):
            p_sum = p_sum + p_c[:, j * tq:(j + 1) * tq]
        ov_t = ovt_ref[...]
        n_blk = ov_t.shape[0]
        p_hi = p_sum.astype(_BF16)
        r1 = p_sum - p_hi.astype(_F32)
        p_mid = r1.astype(_BF16)
        p_lo = (r1 - p_mid.astype(_F32)).astype(_BF16)
        imp = _dot(ov_t, p_hi) + _dot(ov_t, p_mid) + _dot(ov_t, p_lo)

        jsel = lax.broadcasted_iota(jnp.int32, (n_blk, 1), 0)
        cur = qpos_row >> SEL_SHIFT
        forced = (jsel == 0) | (jsel == cur) | (jsel == cur - 1)
        future = jsel * SEL_BLOCK > qpos_row
        work = jnp.where(future, -1.0, jnp.where(forced, FORCED_SCORE, imp))

        jsel_f = jsel.astype(_F32)
        chosen = jnp.zeros((n_blk, tq), _F32)
        for _ in range(SEL_TOPK):
            best = jnp.max(work, axis=0, keepdims=True)
            first = jnp.min(jnp.where(work == best, jsel_f, float(SEL_PAD)), axis=0, keepdims=True)
            pick = jsel_f == first
            chosen = jnp.where(pick, 1.0, chosen)
            work = jnp.where(pick, -jnp.inf, work)
        if n_blk < SEL_PAD:
            chosen = jnp.concatenate([chosen, jnp.ones((SEL_PAD - n_blk, tq), _F32)], axis=0)
        return o_c, chosen

    qs, o_c, q_sel = [], [], []
    for g in groups:
        qs.append(stack_q(g))
        oc, chosen = compressed_and_select(g, qs[g])
        o_c.append(oc)
        penalty = ((chosen - 1.0) * MASKED_SCORE).T.astype(_BF16)
        q_sel.append(jnp.concatenate([qs[g], per_head(penalty)], axis=1))


    def v_rows(kind, g, keys):
        r0 = (kind * NSA_GROUPS + g) * HEAD_DIM
        ones_rows = jnp.ones((ONES_ROWS, keys.size), _BF16)
        return jnp.concatenate([vt_ref[r0:r0 + HEAD_DIM, keys], ones_rows], axis=0)

    def sel_scores(g, keys):
        k_blk = jnp.concatenate([group_cols(ks_ref, keys, g), blk_ref[keys, :]], axis=1)
        return _dot_nt(k_blk, q_sel[g])

    def sel_chunk(first_key, n_keys, carry, bias):
        keys = pl.ds(pl.multiple_of(first_key, tq), n_keys)
        s = [sel_scores(g, keys) for g in groups]
        if bias is not None:
            s = [x + bias for x in s]
        m_new = [jnp.max(x, axis=0, keepdims=True) for x in s]
        if carry is not None:
            m_new = [jnp.maximum(carry[g][0], m_new[g]) for g in groups]
        p = [jnp.exp((s[g] - m_new[g]).astype(_BF16)) for g in groups]
        res = [_dot(v_rows(0, g, keys), p[g]) for g in groups]
        if carry is not None:
            res = [jnp.exp(carry[g][0] - m_new[g]) * carry[g][1] + res[g] for g in groups]
        return tuple((m_new[g], res[g]) for g in groups)

    wlen = WINDOW + tq
    w0 = jnp.maximum(q0 - WINDOW, 0)
    wkeys = pl.ds(pl.multiple_of(w0, tq), wlen)
    key_w = lax.broadcasted_iota(jnp.int32, (wlen, tq), 0)
    qry_w = lax.broadcasted_iota(jnp.int32, (wlen, tq), 1) + (q0 - w0)
    band = per_head_cols(
        jnp.where(key_w <= qry_w, jnp.where(key_w > qry_w - WINDOW, 0.0, -jnp.inf), -jnp.inf).astype(_F32))
    s_w = [_dot_nt(group_cols(kw_ref, wkeys, g), qs[g]) + band for g in groups]
    p_w = [jnp.exp((x - jnp.max(x, axis=0, keepdims=True)).astype(_BF16)) for x in s_w]
    res_w = [_dot(v_rows(1, g, wkeys), p_w[g]) for g in groups]

    key_t = lax.broadcasted_iota(jnp.int32, (tq, tq), 0)
    qry_t = lax.broadcasted_iota(jnp.int32, (tq, tq), 1)
    tri = per_head_cols(_mask_bias(key_t <= qry_t))
    carry = lax.cond(i % 2 == 1,
                     lambda: sel_chunk(q0 - tq, 2 * tq, None, jnp.concatenate([jnp.zeros_like(tri), tri], axis=0)),
                     lambda: sel_chunk(q0, tq, None, tri))
    carry = lax.fori_loop(0, i // 2, lambda j, c: sel_chunk(j * (2 * tq), 2 * tq, c, None), carry)

    def normalised_rows(res):
        return res[:HEAD_DIM] * (1.0 / res[HEAD_DIM:HEAD_DIM + 1])

    for g in groups:
        gate_t = gate_ref[:, g * LANES:(g + 1) * LANES].T
        o_s = normalised_rows(carry[g][1])
        o_w = normalised_rows(res_w[g])
        mixed = []
        for j in range(hpg):
            cols = slice(j * tq, (j + 1) * tq)
            mixed.append(gate_t[3 * j:3 * j + 1] * o_c[g][:, cols] + gate_t[3 * j + 1:3 * j + 2] * o_s[:, cols]
                         + gate_t[3 * j + 2:3 * j + 3] * o_w[:, cols])
        for c in range(hpg // 2):
            pair = jnp.concatenate([mixed[2 * c], mixed[2 * c + 1]], axis=0)
            col = (g * (hpg // 2) + c) * LANES
            o_ref[:, col:col + LANES] = pair.T.astype(o_ref.dtype)


def _overlap_matrix(seq):
    rows = seq // CMP_STRIDE
    n_cmp = (seq - CMP_BLOCK) // CMP_STRIDE + 1
    n_sel = seq // SEL_BLOCK
    cs = np.arange(rows)[:, None] * CMP_STRIDE
    ss = np.arange(SEL_PAD)[None, :] * SEL_BLOCK
    ov = np.clip(np.minimum(cs + CMP_BLOCK, ss + SEL_BLOCK) - np.maximum(cs, ss), 0, None) / CMP_BLOCK
    ov = ov * (np.arange(rows)[:, None] < n_cmp) * (np.arange(SEL_PAD)[None, :] < n_sel)
    n_blk = _sel_rows(seq)
    return jnp.asarray(ov.T[:n_blk], dtype=_BF16)


def _sel_rows(seq):
    n_sel = seq // SEL_BLOCK
    assert SEL_TOPK <= n_sel <= SEL_PAD
    return min(SEL_PAD, -(-n_sel // BF16_SUBLANES) * BF16_SUBLANES)


def _block_onehot(seq):
    hit = (np.arange(seq)[:, None] >> SEL_SHIFT) == np.arange(SEL_PAD)[None, :]
    return jnp.asarray(hit, dtype=_BF16)


def _nsa(qn, gn, ckv, ckv_t, ks, kw, vt, batch, seq):
    tq = NSA_TILE
    nq = seq // tq
    n = batch * seq
    rows = seq // CMP_STRIDE
    width = NSA_GROUPS * LANES
    assert seq // SEL_BLOCK <= SEL_PAD and WINDOW % tq == 0 and seq >= WINDOW + tq
    assert _VT_DIFF_ROWS % _VT_NSA_ROWS == 0
    qmap = lambda b, i: (b * nq + i, 0)
    kvmap = lambda b, i: (b, 0)
    return pl.pallas_call(
        _nsa_kernel,
        grid=(batch, nq),
        in_specs=[pl.BlockSpec((tq, NSA_HEADS * HEAD_DIM), qmap),
                  pl.BlockSpec((tq, width), qmap),
                  _resident((_sel_rows(seq), rows), lambda b, i: (0, 0)),
                  _resident((seq, SEL_PAD), lambda b, i: (0, 0)),
                  pl.BlockSpec((1, 1, rows, width), lambda b, i: (0, b, 0, 0)),
                  pl.BlockSpec((1, 1, NSA_GROUPS * HEAD_DIM, rows), lambda b, i: (1, b, 0, 0)),
                  pl.BlockSpec((seq, width), kvmap), pl.BlockSpec((seq, width), kvmap),
                  pl.BlockSpec((_VT_NSA_ROWS, seq), lambda b, i: (_VT_DIFF_ROWS // _VT_NSA_ROWS, b))],
        out_specs=pl.BlockSpec((tq, NSA_HEADS * HEAD_DIM), qmap),
        out_shape=jax.ShapeDtypeStruct((n, NSA_HEADS * HEAD_DIM), _BF16),
        compiler_params=_params(2),
        name="nsa",
    )(qn, gn, _overlap_matrix(seq), _block_onehot(seq), ckv, ckv_t, ks, kw, vt)


def _merge_kernel(h_ref, oa_ref, ob_ref, ga_ref, gb_ref, wa_ref, wb_ref, wo_ref, o_ref):
    y = (ga_ref[...].astype(_F32) * _dot(oa_ref[...], wa_ref[...])
         + gb_ref[...].astype(_F32) * _dot(ob_ref[...], wb_ref[...]))
    o_ref[...] = h_ref[...] + _dot(y.astype(_BF16), wo_ref[...])


def _merge(h, oa, ob, ga, gb, wa, wb, wo):
    n = h.shape[0]
    tm = min(TOKEN_TILE, n)

    def row(width):
        return pl.BlockSpec((tm, width), lambda i: (i, 0))

    return pl.pallas_call(
        _merge_kernel,
        grid=(n // tm,),
        in_specs=[row(D_MODEL), row(512), row(512), row(D_MODEL), row(D_MODEL),
                  _resident((512, D_MODEL), lambda i: (0, 0)),
                  _resident((512, D_MODEL), lambda i: (0, 0)),
                  _resident((D_MODEL, D_MODEL), lambda i: (0, 0))],
        out_specs=row(D_MODEL),
        out_shape=jax.ShapeDtypeStruct((n, D_MODEL), _F32),
        compiler_params=_params(1),
        name="merge",
    )(h, oa, ob, ga, gb, wa, wb, wo)


def _dup_groups(w):
    a, b = w[:, :HEAD_DIM], w[:, HEAD_DIM:]
    return jnp.concatenate([a, a, b, b], axis=1)


def _prep_w_in(w):
    col = lambda k: w[:, _OFFS[k]:_OFFS[k + 1]]
    q_d, k_d, v_d, q_n, kc, vc, ks, vs, kw, vw, g_n, g_a, g_b = [col(k) for k in range(13)]
    per_group = NSA_HPG * 3
    pad = jnp.zeros((w.shape[0], LANES - per_group), w.dtype)
    gn = jnp.concatenate([g_n[:, :per_group], pad, g_n[:, per_group:], pad], axis=1)
    out = jnp.concatenate([q_d, k_d, q_n, _dup_groups(ks), _dup_groups(kw), kc, vc, gn, g_a, g_b], axis=1)
    assert out.shape[1] == _C_TOTAL
    w_vt = jnp.concatenate([v_d, vs, vw], axis=1).T
    assert w_vt.shape[0] == _VT_DIFF_ROWS + _VT_NSA_ROWS
    return out.astype(_BF16), w_vt.astype(_BF16)


def _prep_compress(pos, w1, w2):
    half = CMP_BLOCK // 2
    w1 = w1.reshape(2, 2, half, HEAD_DIM, CMP_HIDDEN)
    zeros = jnp.zeros_like(w1)
    per_group = []
    for g in range(NSA_GROUPS):
        lanes = [zeros, zeros]
        lanes[g] = w1
        per_group.append(jnp.concatenate(lanes, axis=3))
    w1_x = jnp.stack(per_group, axis=1).reshape(2, NSA_GROUPS, 2, half * LANES, CMP_HIDDEN).astype(_BF16)
    pos = pos.reshape(2, 2, half, HEAD_DIM)
    pos_x = jnp.concatenate([pos, pos], axis=3).reshape(2, 2, 1, half * LANES).astype(_F32)
    w2_x = jnp.concatenate([w2, w2], axis=2).astype(_BF16)
    w2_t = jnp.swapaxes(w2, 1, 2).astype(_BF16)
    return pos_x, w1_x, w2_x, w2_t


def kernel(x, positions, ffn1_norm, ffn1_w_gu, ffn1_w_down, mix_norm, w_in, diff_lambda, cmp_pos, cmp_w1, cmp_w2, w_branch_a, w_branch_b, w_out, ffn2_norm, ffn2_w_gu, ffn2_w_down, final_norm):
    batch, seq, _ = x.shape
    n = batch * seq
    depth = w_in.shape[0]
    assert seq % max(NSA_TILE, DIFF_TILE, CMP_STRIDE * 8) == 0
    cos_t, sin_t = _rope_tables(positions)
    h = x.reshape(n, D_MODEL)
    for l in range(depth):
        bf = lambda a: a.astype(_BF16)
        h = _ffn(h, ffn1_norm[l], bf(ffn1_w_gu[l]), bf(ffn1_w_down[l]))
        (qd, kd, qn, ks, kw, vt, craw, gn, ga, gb) = _mix_in(h, mix_norm[l], cos_t, sin_t, *_prep_w_in(w_in[l]))
        ckv, ckv_t = _compress(craw, *_prep_compress(cmp_pos[l], cmp_w1[l], cmp_w2[l]), batch, seq)
        lam_init = 0.8 - 0.6 * math.exp(-0.3 * l)
        o_a = _diff_attn(diff_lambda[l].astype(_F32), qd, kd, vt, batch, seq, lam_init)
        o_b = _nsa(qn, gn, ckv, ckv_t, ks, kw, vt, batch, seq)
        h = _merge(h, o_a, o_b, ga, gb, bf(w_branch_a[l]), bf(w_branch_b[l]), bf(w_out[l]))
        last = l == depth - 1
        h = _ffn(h, ffn2_norm[l], bf(ffn2_w_gu[l]), bf(ffn2_w_down[l]), final_g=final_norm if last else None)
    return h.reshape(batch, seq, D_MODEL)
```

```python
import functools
import math

import numpy as np
import jax
import jax.numpy as jnp
from jax import lax
from jax.experimental import pallas as pl
from jax.experimental.pallas import tpu as pltpu

D_MODEL = 1024
D_FF = 2816
HEAD_DIM = 64
ROPE_DIM = HEAD_DIM // 4
ROPE_HALF = ROPE_DIM // 2
ROPE_THETA = 500000.0
EPS = 1e-6
DIFF_HEADS = 4
NSA_HEADS = 8
NSA_GROUPS = 2
NSA_HPG = NSA_HEADS // NSA_GROUPS
CMP_BLOCK = 32
CMP_STRIDE = 16
CMP_HIDDEN = 256
SEL_BLOCK = 64
SEL_SHIFT = SEL_BLOCK.bit_length() - 1
SEL_TOPK = 8
WINDOW = 512
FORCED_SCORE = 1.0e4
MASKED_SCORE = 2.0 ** 100
ATTN_SCALE = HEAD_DIM ** -0.5

_SPLITS = (512, 512, 512, 512, 128, 128, 128, 128, 128, 128, 24, 1024, 1024)
_OFFS = np.concatenate([[0], np.cumsum(_SPLITS)]).tolist()

LANES = 128
VMEM_LIMIT_BYTES = 56 * 1024 * 1024
TOKEN_TILE = 512
DIFF_TILE = 256
NSA_TILE = 256
NSA_WINDOW_SPLIT = 2
SEL_PAD = 128

_C_QD, _C_KD, _C_QN, _C_KS, _C_KW = 0, 512, 1024, 1536, 1792
_C_ROPE_END = 2048
_C_CRAW, _C_GN, _C_GA, _C_GB = 2048, 2304, 2560, 3584
_C_TOTAL = 4608
_VT_DIFF_ROWS = 512
_VT_NSA_ROWS = 2 * NSA_GROUPS * HEAD_DIM
BF16_SUBLANES = 16
ONES_ROWS = BF16_SUBLANES

_BF16 = jnp.bfloat16
_F32 = jnp.float32


def _params(n_grid):
    return pltpu.CompilerParams(
        dimension_semantics=("arbitrary",) * n_grid,
        vmem_limit_bytes=VMEM_LIMIT_BYTES,
    )


def _resident(shape, index_map):
    return pl.BlockSpec(shape, index_map, pipeline_mode=pl.Buffered(1))


def _dot(a, b):
    return jnp.dot(a, b, preferred_element_type=_F32)


def _dot_nt(a, b):
    return lax.dot_general(a, b, (((1,), (1,)), ((), ())), preferred_element_type=_F32)


def _sigmoid(x):
    return 1.0 / (1.0 + jnp.exp(-x))


def _rms_scale(x):
    return x * lax.rsqrt(jnp.mean(x * x, axis=-1, keepdims=True) + EPS)


def _rope_table_kernel(pos_ref, cos_ref, sin_ref):
    lane = lax.broadcasted_iota(jnp.int32, (1, LANES), 1)
    in_head = lane & (HEAD_DIM - 1)
    freq = (in_head & (ROPE_HALF - 1)).astype(_F32)
    inv = jnp.exp(freq * (-2.0 / ROPE_DIM * math.log(ROPE_THETA)))
    inv = jnp.where(in_head < ROPE_DIM, inv, 0.0)
    ang = pos_ref[...].astype(_F32) * inv
    cos_ref[...] = jnp.cos(ang)
    sin = jnp.sin(ang)
    sin_ref[...] = jnp.where(in_head < ROPE_HALF, -sin, sin)


def _rope_tables(positions):
    n = positions.size
    tm = min(TOKEN_TILE, n)
    pos = positions.reshape(n, 1)
    return pl.pallas_call(
        _rope_table_kernel,
        grid=(n // tm,),
        in_specs=[pl.BlockSpec((tm, 1), lambda i: (i, 0))],
        out_specs=[pl.BlockSpec((tm, LANES), lambda i: (i, 0))] * 2,
        out_shape=[jax.ShapeDtypeStruct((n, LANES), _F32)] * 2,
        compiler_params=_params(1),
        name="rope_tables",
    )(pos)


def _ffn_kernel(h_ref, g_ref, wg_ref, wu_ref, wd_ref, *rest, final):
    if final:
        fg_ref, o_ref = rest
    else:
        (o_ref,) = rest
    x = h_ref[...]
    u = (_rms_scale(x) * g_ref[...]).astype(_BF16)
    gate = _dot(u, wg_ref[...])
    up = _dot(u, wu_ref[...])
    act = (gate * _sigmoid(gate) * up).astype(_BF16)
    y = x + 0.5 * _dot(act, wd_ref[...])
    if final:
        y = _rms_scale(y) * fg_ref[...]
    o_ref[...] = y


def _ffn(h, g, w_gu, wd, final_g=None):
    n = h.shape[0]
    tm = min(TOKEN_TILE, n)
    final = final_g is not None
    row = pl.BlockSpec((tm, D_MODEL), lambda i: (i, 0))
    vec = _resident((1, D_MODEL), lambda i: (0, 0))
    in_specs = [row, vec,
                _resident((D_MODEL, D_FF), lambda i: (0, 0)),
                _resident((D_MODEL, D_FF), lambda i: (0, 1)),
                _resident((D_FF, D_MODEL), lambda i: (0, 0))]
    args = [h, g.reshape(1, D_MODEL), w_gu, w_gu, wd]
    if final:
        in_specs.append(vec)
        args.append(final_g.reshape(1, D_MODEL))
    return pl.pallas_call(
        functools.partial(_ffn_kernel, final=final),
        grid=(n // tm,),
        in_specs=in_specs,
        out_specs=row,
        out_shape=jax.ShapeDtypeStruct((n, D_MODEL), _F32),
        compiler_params=_params(1),
        name="ffn_final" if final else "ffn",
    )(*args)


def _mix_in_kernel(h_ref, g_ref, cos_ref, sin_ref, w_ref, wvt_ref,
                   qd_ref, kd_ref, qn_ref, ks_ref, kw_ref, vt_ref,
                   craw_ref, gn_ref, ga_ref, gb_ref, craw_scr):
    x = h_ref[...]
    u = (_rms_scale(x) * g_ref[...]).astype(_BF16)
    cos = cos_ref[...]
    sin = sin_ref[...]
    lane = lax.broadcasted_iota(jnp.int32, (1, LANES), 1)
    first_half = (lane & (HEAD_DIM - 1)) < ROPE_HALF

    def proj(c0, width):
        return _dot(u, w_ref[:, c0:c0 + width])

    def rope_store(o_ref, c0, width, scale):
        z = proj(c0, width)
        for c in range(width // LANES):
            zc = z[:, c * LANES:(c + 1) * LANES]
            partner = jnp.where(first_half, pltpu.roll(zc, LANES - ROPE_HALF, axis=1),
                                pltpu.roll(zc, ROPE_HALF, axis=1))
            r = zc * cos + partner * sin
            if scale != 1.0:
                r = r * scale
            o_ref[:, c * LANES:(c + 1) * LANES] = r.astype(o_ref.dtype)

    rope_store(qd_ref, _C_QD, 512, ATTN_SCALE)
    rope_store(kd_ref, _C_KD, 512, 1.0)
    rope_store(qn_ref, _C_QN, 512, ATTN_SCALE)
    rope_store(ks_ref, _C_KS, 256, 1.0)
    rope_store(kw_ref, _C_KW, 256, 1.0)
    vt_ref[...] = _dot_nt(wvt_ref[...], u).astype(_BF16)
    craw = proj(_C_CRAW, 256)
    for kind in range(2):
        craw_scr[kind] = craw[:, kind * LANES:(kind + 1) * LANES]
        for t in range(CMP_STRIDE):
            every = pl.ds(t, craw_scr.shape[1] // CMP_STRIDE, stride=CMP_STRIDE)
            craw_ref[kind, :, t * LANES:(t + 1) * LANES] = craw_scr[kind, every, :]
    gn_ref[...] = _sigmoid(proj(_C_GN, 256))
    ga_ref[...] = _sigmoid(proj(_C_GA, D_MODEL)).astype(_BF16)
    gb_ref[...] = _sigmoid(proj(_C_GB, D_MODEL)).astype(_BF16)


def _mix_in(h, g, cos_t, sin_t, w, wvt):
    n = h.shape[0]
    tm = min(TOKEN_TILE, n)
    vd_width = wvt.shape[0]

    def row(width):
        return pl.BlockSpec((tm, width), lambda i: (i, 0))

    def bf16_rows(width):
        return row(width), jax.ShapeDtypeStruct((n, width), _BF16)

    outs = [bf16_rows(512), bf16_rows(512), bf16_rows(512), bf16_rows(256), bf16_rows(256),
            (pl.BlockSpec((vd_width, tm), lambda i: (0, i)), jax.ShapeDtypeStruct((vd_width, n), _BF16))]
    out_specs = [spec for spec, _ in outs]
    out_shape = [shape for _, shape in outs]
    assert tm % (CMP_STRIDE * 8) == 0
    out_specs += [pl.BlockSpec((2, tm // CMP_STRIDE, CMP_STRIDE * LANES), lambda i: (0, i, 0)),
                  row(256), row(D_MODEL), row(D_MODEL)]
    out_shape += [jax.ShapeDtypeStruct((2, n // CMP_STRIDE, CMP_STRIDE * LANES), _F32),
                  jax.ShapeDtypeStruct((n, 256), _F32),
                  jax.ShapeDtypeStruct((n, D_MODEL), _BF16),
                  jax.ShapeDtypeStruct((n, D_MODEL), _BF16)]
    return pl.pallas_call(
        _mix_in_kernel,
        grid=(n // tm,),
        in_specs=[row(D_MODEL), _resident((1, D_MODEL), lambda i: (0, 0)),
                  row(LANES), row(LANES),
                  _resident((D_MODEL, _C_TOTAL), lambda i: (0, 0)),
                  _resident((vd_width, D_MODEL), lambda i: (0, 0))],
        out_specs=out_specs,
        out_shape=out_shape,
        scratch_shapes=[pltpu.VMEM((2, tm, LANES), _F32)],
        compiler_params=_params(1),
        name="mix_in",
    )(h, g.reshape(1, D_MODEL), cos_t, sin_t, w, wvt)


def _compress_kernel(x_ref, pos_ref, w1_ref, w2_ref, w2t_ref, o_ref, ot_ref, *, n_cmp):
    x = x_ref[0, 0]
    rows = x.shape[0]
    x_top = (x + pos_ref[0, 0]).astype(_BF16)
    x_bot = (x + pos_ref[0, 1]).astype(_BF16)
    valid = lax.broadcasted_iota(jnp.int32, (rows, 1), 0) < n_cmp
    valid_t = lax.broadcasted_iota(jnp.int32, (1, rows), 1) < n_cmp
    for g in range(NSA_GROUPS):
        top = _dot(x_top, w1_ref[0, g, 0])
        bot = _dot(x_bot, w1_ref[0, g, 1])
        pre = top + pltpu.roll(bot, rows - 1, axis=0)
        hid = (pre * _sigmoid(pre)).astype(_BF16)
        out = _dot(hid, w2_ref[0])
        out = jnp.where(valid, out, 0.0)
        o_ref[0, 0, :, g * LANES:(g + 1) * LANES] = out.astype(o_ref.dtype)
        out_t = jnp.where(valid_t, _dot_nt(w2t_ref[0], hid), 0.0)
        ot_ref[0, 0, g * HEAD_DIM:(g + 1) * HEAD_DIM, :] = out_t.astype(ot_ref.dtype)


def _compress(craw, pos_x, w1_x, w2_x, w2_t, batch, seq):
    rows = seq // CMP_STRIDE
    n_cmp = (seq - CMP_BLOCK) // CMP_STRIDE + 1
    width = CMP_STRIDE * LANES
    x = craw.reshape(2, batch, rows, width)
    return pl.pallas_call(
        functools.partial(_compress_kernel, n_cmp=n_cmp),
        grid=(2, batch),
        in_specs=[pl.BlockSpec((1, 1, rows, width), lambda k, b: (k, b, 0, 0)),
                  pl.BlockSpec((1, 2, 1, width), lambda k, b: (k, 0, 0, 0)),
                  pl.BlockSpec((1, NSA_GROUPS, 2, width, CMP_HIDDEN), lambda k, b: (k, 0, 0, 0, 0)),
                  pl.BlockSpec((1, CMP_HIDDEN, LANES), lambda k, b: (k, 0, 0)),
                  pl.BlockSpec((1, HEAD_DIM, CMP_HIDDEN), lambda k, b: (k, 0, 0))],
        out_specs=[pl.BlockSpec((1, 1, rows, NSA_GROUPS * LANES), lambda k, b: (k, b, 0, 0)),
                   pl.BlockSpec((1, 1, NSA_GROUPS * HEAD_DIM, rows), lambda k, b: (k, b, 0, 0))],
        out_shape=[jax.ShapeDtypeStruct((2, batch, rows, NSA_GROUPS * LANES), _BF16),
                   jax.ShapeDtypeStruct((2, batch, NSA_GROUPS * HEAD_DIM, rows), _BF16)],
        compiler_params=_params(2),
        name="compress",
    )(x, pos_x, w1_x, w2_x, w2_t)


def _mask_bias(mask):
    return jnp.where(mask, 0.0, -jnp.inf).astype(_F32)


def _diff_attn_kernel(lam_ref, q_ref, k_ref, vt_ref, o_ref, *, lam_init):
    i = pl.program_id(1)
    tq = DIFF_TILE
    lane = lax.broadcasted_iota(jnp.int32, (1, LANES), 1)
    heads = range(DIFF_HEADS)

    qs = []
    for h in heads:
        q = q_ref[:, h * LANES:(h + 1) * LANES]
        zero = jnp.zeros_like(q)
        qs.append(jnp.concatenate([jnp.where(lane < HEAD_DIM, q, zero), jnp.where(lane >= HEAD_DIM, q, zero)], axis=0))

    def scores(h, keys):
        return _dot_nt(k_ref[keys, h * LANES:(h + 1) * LANES], qs[h])

    def weighted(h, p, keys):
        ones_rows = jnp.ones((ONES_ROWS, keys.size), _BF16)
        v_one = jnp.concatenate([vt_ref[h * LANES:(h + 1) * LANES, keys], ones_rows], axis=0)
        return _dot(v_one, p)

    def chunk(first_key, n_keys, carry, bias):
        keys = pl.ds(pl.multiple_of(first_key, tq), n_keys)
        s = [scores(h, keys) for h in heads]
        if bias is not None:
            s = [x + bias for x in s]
        m_new = [jnp.max(x, axis=0, keepdims=True) for x in s]
        if carry is not None:
            m_new = [jnp.maximum(carry[h][0], m_new[h]) for h in heads]
        p = [jnp.exp((s[h] - m_new[h]).astype(_BF16)) for h in heads]
        res = [weighted(h, p[h], keys) for h in heads]
        if carry is not None:
            res = [jnp.exp(carry[h][0] - m_new[h]) * carry[h][1] + res[h] for h in heads]
        return tuple((m_new[h], res[h]) for h in heads)

    key = lax.broadcasted_iota(jnp.int32, (tq, tq), 0)
    qry = lax.broadcasted_iota(jnp.int32, (tq, tq), 1)
    tri = _mask_bias(key <= qry)
    tri = jnp.concatenate([tri, tri], axis=1)
    carry = lax.cond(i % 2 == 1,
                     lambda: chunk((i - 1) * tq, 2 * tq, None, jnp.concatenate([jnp.zeros_like(tri), tri], axis=0)),
                     lambda: chunk(i * tq, tq, None, tri))
    carry = lax.fori_loop(0, i // 2, lambda j, c: chunk(j * (2 * tq), 2 * tq, c, None), carry)
    lp = lam_ref[...]
    lam = (jnp.exp(jnp.sum(lp[0:1] * lp[1:2], axis=1, keepdims=True))
           - jnp.exp(jnp.sum(lp[2:3] * lp[3:4], axis=1, keepdims=True)) + lam_init)
    for h in heads:
        _, acc = carry[h]
        o = acc[:LANES] * (1.0 / acc[LANES:LANES + 1])
        o = o[:, :tq] - lam * o[:, tq:]
        o = o * lax.rsqrt(jnp.mean(o * o, axis=0, keepdims=True) + EPS) * (1.0 - lam_init)
        o_ref[:, h * LANES:(h + 1) * LANES] = o.T.astype(o_ref.dtype)


def _diff_attn(lam_p, qd, kd, vdt, batch, seq, lam_init):
    tq = DIFF_TILE
    nq = seq // tq
    n = batch * seq
    width = DIFF_HEADS * LANES
    return pl.pallas_call(
        functools.partial(_diff_attn_kernel, lam_init=lam_init),
        grid=(batch, nq),
        in_specs=[pl.BlockSpec((4, HEAD_DIM), lambda b, i: (0, 0)),
                  pl.BlockSpec((tq, width), lambda b, i: (b * nq + i, 0)),
                  pl.BlockSpec((seq, width), lambda b, i: (b, 0)),
                  pl.BlockSpec((width, seq), lambda b, i: (0, b))],
        out_specs=pl.BlockSpec((tq, width), lambda b, i: (b * nq + i, 0)),
        out_shape=jax.ShapeDtypeStruct((n, width), _BF16),
        compiler_params=_params(2),
        name="diff_attn",
    )(lam_p, qd, kd, vdt)


def _nsa_kernel(q_ref, gate_ref, ovt_ref, blk_ref, kc_ref, vct_ref, ks_ref, kw_ref, vt_ref, o_ref):
    i = pl.program_id(1)
    tq = NSA_TILE
    hpg = NSA_HPG
    groups = range(NSA_GROUPS)
    lane = lax.broadcasted_iota(jnp.int32, (1, LANES), 1)
    low = lane < HEAD_DIM
    q0 = i * tq

    def per_head(x):
        return jnp.concatenate([x] * hpg, axis=0)

    def group_cols(ref, rows, g):
        return ref[rows, g * LANES:(g + 1) * LANES]

    def stack_q(g):
        parts = []
        for j in range(hpg):
            c0 = g * hpg * HEAD_DIM + (j // 2) * LANES
            qc = q_ref[:, c0:c0 + LANES]
            keep = low if j % 2 == 0 else jnp.logical_not(low)
            parts.append(jnp.where(keep, qc, jnp.zeros_like(qc)))
        return jnp.concatenate(parts, axis=0)

    def per_head_cols(x):
        return jnp.concatenate([x] * hpg, axis=1)

    qpos_row = q0 + lax.broadcasted_iota(jnp.int32, (1, tq), 1)

    def compressed_and_select(g, qs):
        kc = kc_ref[0, 0, :, g * LANES:(g + 1) * LANES]
        vc_t = vct_ref[0, 0, g * HEAD_DIM:(g + 1) * HEAD_DIM, :]
        ncp = kc.shape[0]
        s_c = _dot_nt(kc, qs)
        cmp_end = lax.broadcasted_iota(jnp.int32, (ncp, 1), 0) * CMP_STRIDE + (CMP_BLOCK - 1)
        s_c = s_c + per_head_cols(_mask_bias(cmp_end <= qpos_row))
        mx = jnp.max(s_c, axis=0, keepdims=True)
        mx = jnp.where(mx == -jnp.inf, 0.0, mx)
        p_c = jnp.exp(s_c - mx)
        p_c = p_c * (1.0 / jnp.maximum(jnp.sum(p_c, axis=0, keepdims=True), 1e-30))
        o_c = _dot(vc_t, p_c.astype(_BF16))

        p_sum = p_c[:, 0:tq]
        for j in range(1, hpg):
            p_sum = p_sum + p_c[:, j * tq:(j + 1) * tq]
        ov_t = ovt_ref[...]
        n_blk = ov_t.shape[0]
        p_hi = p_sum.astype(_BF16)
        r1 = p_sum - p_hi.astype(_F32)
        p_mid = r1.astype(_BF16)
        p_lo = (r1 - p_mid.astype(_F32)).astype(_BF16)
        imp = _dot(ov_t, p_hi) + _dot(ov_t, p_mid) + _dot(ov_t, p_lo)

        jsel = lax.broadcasted_iota(jnp.int32, (n_blk, 1), 0)
        cur = qpos_row >> SEL_SHIFT
        forced = (jsel == 0) | (jsel == cur) | (jsel == cur - 1)
        future = jsel * SEL_BLOCK > qpos_row
        work = jnp.where(future, -1.0, jnp.where(forced, FORCED_SCORE, imp))

        jsel_f = jsel.astype(_F32)
        chosen = jnp.zeros((n_blk, tq), _F32)
        for _ in range(SEL_TOPK):
            best = jnp.max(work, axis=0, keepdims=True)
            first = jnp.min(jnp.where(work == best, jsel_f, float(SEL_PAD)), axis=0, keepdims=True)
            pick = jsel_f == first
            chosen = jnp.where(pick, 1.0, chosen)
            work = jnp.where(pick, -jnp.inf, work)
        if n_blk < SEL_PAD:
            chosen = jnp.concatenate([chosen, jnp.ones((SEL_PAD - n_blk, tq), _F32)], axis=0)
        return o_c, chosen

    qs, o_c, q_sel = [], [], []
    for g in groups:
        qs.append(stack_q(g))
        oc, chosen = compressed_and_select(g, qs[g])
        o_c.append(oc)
        penalty = ((chosen - 1.0) * MASKED_SCORE).T.astype(_BF16)
        q_sel.append(jnp.concatenate([qs[g], per_head(penalty)], axis=1))


    def v_rows(kind, g, keys):
        r0 = (kind * NSA_GROUPS + g) * HEAD_DIM
        ones_rows = jnp.ones((ONES_ROWS, keys.size), _BF16)
        return jnp.concatenate([vt_ref[r0:r0 + HEAD_DIM, keys], ones_rows], axis=0)

    def sel_scores(g, keys):
        k_blk = jnp.concatenate([group_cols(ks_ref, keys, g), blk_ref[keys, :]], axis=1)
        return _dot_nt(k_blk, q_sel[g])

    def sel_chunk(first_key, n_keys, carry, bias):
        keys = pl.ds(pl.multiple_of(first_key, tq), n_keys)
        s = [sel_scores(g, keys) for g in groups]
        if bias is not None:
            s = [x + bias for x in s]
        m_new = [jnp.max(x, axis=0, keepdims=True) for x in s]
        if carry is not None:
            m_new = [jnp.maximum(carry[g][0], m_new[g]) for g in groups]
        p = [jnp.exp((s[g] - m_new[g]).astype(_BF16)) for g in groups]
        res = [_dot(v_rows(0, g, keys), p[g]) for g in groups]
        if carry is not None:
            res = [jnp.exp(carry[g][0] - m_new[g]) * carry[g][1] + res[g] for g in groups]
        return tuple((m_new[g], res[g]) for g in groups)

    sub = tq // NSA_WINDOW_SPLIT
    wlen = WINDOW + sub
    key_w = lax.broadcasted_iota(jnp.int32, (wlen, sub), 0)
    qry_w = lax.broadcasted_iota(jnp.int32, (wlen, sub), 1)
    res_sub = []
    for t in range(NSA_WINDOW_SPLIT):
        w0 = jnp.maximum(q0 + t * sub - WINDOW, 0)
        wkeys = pl.ds(pl.multiple_of(w0, sub), wlen)
        qry = qry_w + (q0 + t * sub - w0)
        band = per_head_cols(
            jnp.where(key_w <= qry, jnp.where(key_w > qry - WINDOW, 0.0, -jnp.inf), -jnp.inf).astype(_F32))
        q_sub = [jnp.concatenate([qs[g][j * tq + t * sub:j * tq + (t + 1) * sub] for j in range(hpg)], axis=0)
                 for g in groups]
        s_w = [_dot_nt(group_cols(kw_ref, wkeys, g), q_sub[g]) + band for g in groups]
        p_w = [jnp.exp((x - jnp.max(x, axis=0, keepdims=True)).astype(_BF16)) for x in s_w]
        res_sub.append([_dot(v_rows(1, g, wkeys), p_w[g]) for g in groups])
    res_w = [jnp.concatenate([res_sub[t][g][:, j * sub:(j + 1) * sub]
                              for j in range(hpg) for t in range(NSA_WINDOW_SPLIT)], axis=1) for g in groups]

    key_t = lax.broadcasted_iota(jnp.int32, (tq, tq), 0)
    qry_t = lax.broadcasted_iota(jnp.int32, (tq, tq), 1)
    tri = per_head_cols(_mask_bias(key_t <= qry_t))
    carry = lax.cond(i % 2 == 1,
                     lambda: sel_chunk(q0 - tq, 2 * tq, None, jnp.concatenate([jnp.zeros_like(tri), tri], axis=0)),
                     lambda: sel_chunk(q0, tq, None, tri))
    carry = lax.fori_loop(0, i // 2, lambda j, c: sel_chunk(j * (2 * tq), 2 * tq, c, None), carry)

    def normalised_rows(res):
        return res[:HEAD_DIM] * (1.0 / res[HEAD_DIM:HEAD_DIM + 1])

    for g in groups:
        gate_t = gate_ref[:, g * LANES:(g + 1) * LANES].T
        o_s = normalised_rows(carry[g][1])
        o_w = normalised_rows(res_w[g])
        mixed = []
        for j in range(hpg):
            cols = slice(j * tq, (j + 1) * tq)
            mixed.append(gate_t[3 * j:3 * j + 1] * o_c[g][:, cols] + gate_t[3 * j + 1:3 * j + 2] * o_s[:, cols]
                         + gate_t[3 * j + 2:3 * j + 3] * o_w[:, cols])
        for c in range(hpg // 2):
            pair = jnp.concatenate([mixed[2 * c], mixed[2 * c + 1]], axis=0)
            col = (g * (hpg // 2) + c) * LANES
            o_ref[:, col:col + LANES] = pair.T.astype(o_ref.dtype)


def _overlap_matrix(seq):
    rows = seq // CMP_STRIDE
    n_cmp = (seq - CMP_BLOCK) // CMP_STRIDE + 1
    n_sel = seq // SEL_BLOCK
    cs = np.arange(rows)[:, None] * CMP_STRIDE
    ss = np.arange(SEL_PAD)[None, :] * SEL_BLOCK
    ov = np.clip(np.minimum(cs + CMP_BLOCK, ss + SEL_BLOCK) - np.maximum(cs, ss), 0, None) / CMP_BLOCK
    ov = ov * (np.arange(rows)[:, None] < n_cmp) * (np.arange(SEL_PAD)[None, :] < n_sel)
    n_blk = _sel_rows(seq)
    return jnp.asarray(ov.T[:n_blk], dtype=_BF16)


def _sel_rows(seq):
    n_sel = seq // SEL_BLOCK
    assert SEL_TOPK <= n_sel <= SEL_PAD
    return min(SEL_PAD, -(-n_sel // BF16_SUBLANES) * BF16_SUBLANES)


def _block_onehot(seq):
    hit = (np.arange(seq)[:, None] >> SEL_SHIFT) == np.arange(SEL_PAD)[None, :]
    return jnp.asarray(hit, dtype=_BF16)


def _nsa(qn, gn, ckv, ckv_t, ks, kw, vt, batch, seq):
    tq = NSA_TILE
    nq = seq // tq
    n = batch * seq
    rows = seq // CMP_STRIDE
    width = NSA_GROUPS * LANES
    assert seq // SEL_BLOCK <= SEL_PAD and seq >= WINDOW + tq
    assert tq % NSA_WINDOW_SPLIT == 0 and WINDOW % (tq // NSA_WINDOW_SPLIT) == 0
    assert _VT_DIFF_ROWS % _VT_NSA_ROWS == 0
    qmap = lambda b, i: (b * nq + i, 0)
    kvmap = lambda b, i: (b, 0)
    return pl.pallas_call(
        _nsa_kernel,
        grid=(batch, nq),
        in_specs=[pl.BlockSpec((tq, NSA_HEADS * HEAD_DIM), qmap),
                  pl.BlockSpec((tq, width), qmap),
                  _resident((_sel_rows(seq), rows), lambda b, i: (0, 0)),
                  _resident((seq, SEL_PAD), lambda b, i: (0, 0)),
                  pl.BlockSpec((1, 1, rows, width), lambda b, i: (0, b, 0, 0)),
                  pl.BlockSpec((1, 1, NSA_GROUPS * HEAD_DIM, rows), lambda b, i: (1, b, 0, 0)),
                  pl.BlockSpec((seq, width), kvmap), pl.BlockSpec((seq, width), kvmap),
                  pl.BlockSpec((_VT_NSA_ROWS, seq), lambda b, i: (_VT_DIFF_ROWS // _VT_NSA_ROWS, b))],
        out_specs=pl.BlockSpec((tq, NSA_HEADS * HEAD_DIM), qmap),
        out_shape=jax.ShapeDtypeStruct((n, NSA_HEADS * HEAD_DIM), _BF16),
        compiler_params=_params(2),
        name="nsa",
    )(qn, gn, _overlap_matrix(seq), _block_onehot(seq), ckv, ckv_t, ks, kw, vt)


def _merge_kernel(h_ref, oa_ref, ob_ref, ga_ref, gb_ref, wa_ref, wb_ref, wo_ref, o_ref):
    y = (ga_ref[...].astype(_F32) * _dot(oa_ref[...], wa_ref[...])
         + gb_ref[...].astype(_F32) * _dot(ob_ref[...], wb_ref[...]))
    o_ref[...] = h_ref[...] + _dot(y.astype(_BF16), wo_ref[...])


def _merge(h, oa, ob, ga, gb, wa, wb, wo):
    n = h.shape[0]
    tm = min(TOKEN_TILE, n)

    def row(width):
        return pl.BlockSpec((tm, width), lambda i: (i, 0))

    return pl.pallas_call(
        _merge_kernel,
        grid=(n // tm,),
        in_specs=[row(D_MODEL), row(512), row(512), row(D_MODEL), row(D_MODEL),
                  _resident((512, D_MODEL), lambda i: (0, 0)),
                  _resident((512, D_MODEL), lambda i: (0, 0)),
                  _resident((D_MODEL, D_MODEL), lambda i: (0, 0))],
        out_specs=row(D_MODEL),
        out_shape=jax.ShapeDtypeStruct((n, D_MODEL), _F32),
        compiler_params=_params(1),
        name="merge",
    )(h, oa, ob, ga, gb, wa, wb, wo)


def _dup_groups(w):
    a, b = w[:, :HEAD_DIM], w[:, HEAD_DIM:]
    return jnp.concatenate([a, a, b, b], axis=1)


def _prep_w_in(w):
    col = lambda k: w[:, _OFFS[k]:_OFFS[k + 1]]
    q_d, k_d, v_d, q_n, kc, vc, ks, vs, kw, vw, g_n, g_a, g_b = [col(k) for k in range(13)]
    per_group = NSA_HPG * 3
    pad = jnp.zeros((w.shape[0], LANES - per_group), w.dtype)
    gn = jnp.concatenate([g_n[:, :per_group], pad, g_n[:, per_group:], pad], axis=1)
    out = jnp.concatenate([q_d, k_d, q_n, _dup_groups(ks), _dup_groups(kw), kc, vc, gn, g_a, g_b], axis=1)
    assert out.shape[1] == _C_TOTAL
    w_vt = jnp.concatenate([v_d, vs, vw], axis=1).T
    assert w_vt.shape[0] == _VT_DIFF_ROWS + _VT_NSA_ROWS
    return out.astype(_BF16), w_vt.astype(_BF16)


def _prep_compress(pos, w1, w2):
    half = CMP_BLOCK // 2
    w1 = w1.reshape(2, 2, half, HEAD_DIM, CMP_HIDDEN)
    zeros = jnp.zeros_like(w1)
    per_group = []
    for g in range(NSA_GROUPS):
        lanes = [zeros, zeros]
        lanes[g] = w1
        per_group.append(jnp.concatenate(lanes, axis=3))
    w1_x = jnp.stack(per_group, axis=1).reshape(2, NSA_GROUPS, 2, half * LANES, CMP_HIDDEN).astype(_BF16)
    pos = pos.reshape(2, 2, half, HEAD_DIM)
    pos_x = jnp.concatenate([pos, pos], axis=3).reshape(2, 2, 1, half * LANES).astype(_F32)
    w2_x = jnp.concatenate([w2, w2], axis=2).astype(_BF16)
    w2_t = jnp.swapaxes(w2, 1, 2).astype(_BF16)
    return pos_x, w1_x, w2_x, w2_t


def kernel(x, positions, ffn1_norm, ffn1_w_gu, ffn1_w_down, mix_norm, w_in, diff_lambda, cmp_pos, cmp_w1, cmp_w2, w_branch_a, w_branch_b, w_out, ffn2_norm, ffn2_w_gu, ffn2_w_down, final_norm):
    batch, seq, _ = x.shape
    n = batch * seq
    depth = w_in.shape[0]
    assert seq % max(NSA_TILE, DIFF_TILE, CMP_STRIDE * 8) == 0
    cos_t, sin_t = _rope_tables(positions)
    h = x.reshape(n, D_MODEL)
    for l in range(depth):
        bf = lambda a: a.astype(_BF16)
        h = _ffn(h, ffn1_norm[l], bf(ffn1_w_gu[l]), bf(ffn1_w_down[l]))
        (qd, kd, qn, ks, kw, vt, craw, gn, ga, gb) = _mix_in(h, mix_norm[l], cos_t, sin_t, *_prep_w_in(w_in[l]))
        ckv, ckv_t = _compress(craw, *_prep_compress(cmp_pos[l], cmp_w1[l], cmp_w2[l]), batch, seq)
        lam_init = 0.8 - 0.6 * math.exp(-0.3 * l)
        o_a = _diff_attn(diff_lambda[l].astype(_F32), qd, kd, vt, batch, seq, lam_init)
        o_b = _nsa(qn, gn, ckv, ckv_t, ks, kw, vt, batch, seq)
        h = _merge(h, o_a, o_b, ga, gb, bf(w_branch_a[l]), bf(w_branch_b[l]), bf(w_out[l]))
        last = l == depth - 1
        h = _ffn(h, ffn2_norm[l], bf(ffn2_w_gu[l]), bf(ffn2_w_down[l]), final_g=final_norm if last else None)
    return h.reshape(batch, seq, D_MODEL)
```

```python
import functools
import math

import numpy as np
import jax
import jax.numpy as jnp
from jax import lax
from jax.experimental import pallas as pl
from jax.experimental.pallas import tpu as pltpu

D_MODEL = 1024
D_FF = 2816
HEAD_DIM = 64
ROPE_DIM = HEAD_DIM // 4
ROPE_HALF = ROPE_DIM // 2
ROPE_THETA = 500000.0
EPS = 1e-6
DIFF_HEADS = 4
NSA_HEADS = 8
NSA_GROUPS = 2
NSA_HPG = NSA_HEADS // NSA_GROUPS
CMP_BLOCK = 32
CMP_STRIDE = 16
CMP_HIDDEN = 256
SEL_BLOCK = 64
SEL_SHIFT = SEL_BLOCK.bit_length() - 1
SEL_TOPK = 8
WINDOW = 512
FORCED_SCORE = 1.0e4
MASKED_SCORE = 2.0 ** 100
ATTN_SCALE = HEAD_DIM ** -0.5

_SPLITS = (512, 512, 512, 512, 128, 128, 128, 128, 128, 128, 24, 1024, 1024)
_OFFS = np.concatenate([[0], np.cumsum(_SPLITS)]).tolist()

LANES = 128
VMEM_LIMIT_BYTES = 56 * 1024 * 1024
TOKEN_TILE = 512
DIFF_TILE = 256
NSA_TILE = 256
NSA_WINDOW_SPLIT = 2
SEL_PAD = 128

_C_QD, _C_KD, _C_QN, _C_KS, _C_KW = 0, 512, 1024, 1536, 1792
_C_ROPE_END = 2048
_C_CRAW, _C_GN, _C_GA, _C_GB = 2048, 2304, 2560, 3584
_C_TOTAL = 4608
_VT_DIFF_ROWS = 512
_VT_NSA_ROWS = 2 * NSA_GROUPS * HEAD_DIM
BF16_SUBLANES = 16
ONES_ROWS = BF16_SUBLANES

_BF16 = jnp.bfloat16
_F32 = jnp.float32


def _params(n_grid):
    return pltpu.CompilerParams(
        dimension_semantics=("arbitrary",) * n_grid,
        vmem_limit_bytes=VMEM_LIMIT_BYTES,
    )


def _resident(shape, index_map):
    return pl.BlockSpec(shape, index_map, pipeline_mode=pl.Buffered(1))


def _dot(a, b):
    return jnp.dot(a, b, preferred_element_type=_F32)


def _dot_nt(a, b):
    return lax.dot_general(a, b, (((1,), (1,)), ((), ())), preferred_element_type=_F32)


def _sigmoid(x):
    return 1.0 / (1.0 + jnp.exp(-x))


def _rms_scale(x):
    return x * lax.rsqrt(jnp.mean(x * x, axis=-1, keepdims=True) + EPS)


def _rope_table_kernel(pos_ref, cos_ref, sin_ref):
    lane = lax.broadcasted_iota(jnp.int32, (1, LANES), 1)
    in_head = lane & (HEAD_DIM - 1)
    freq = (in_head & (ROPE_HALF - 1)).astype(_F32)
    inv = jnp.exp(freq * (-2.0 / ROPE_DIM * math.log(ROPE_THETA)))
    inv = jnp.where(in_head < ROPE_DIM, inv, 0.0)
    ang = pos_ref[...].astype(_F32) * inv
    cos_ref[...] = jnp.cos(ang)
    sin = jnp.sin(ang)
    sin_ref[...] = jnp.where(in_head < ROPE_HALF, -sin, sin)


def _rope_tables(positions):
    n = positions.size
    tm = min(TOKEN_TILE, n)
    pos = positions.reshape(n, 1)
    return pl.pallas_call(
        _rope_table_kernel,
        grid=(n // tm,),
        in_specs=[pl.BlockSpec((tm, 1), lambda i: (i, 0))],
        out_specs=[pl.BlockSpec((tm, LANES), lambda i: (i, 0))] * 2,
        out_shape=[jax.ShapeDtypeStruct((n, LANES), _F32)] * 2,
        compiler_params=_params(1),
        name="rope_tables",
    )(pos)


def _ffn_kernel(h_ref, g_ref, wg_ref, wu_ref, wd_ref, *rest, final):
    if final:
        fg_ref, o_ref = rest
    else:
        (o_ref,) = rest
    x = h_ref[...]
    u = (_rms_scale(x) * g_ref[...]).astype(_BF16)
    gate = _dot(u, wg_ref[...])
    up = _dot(u, wu_ref[...])
    act = (gate * _sigmoid(gate) * up).astype(_BF16)
    y = x + 0.5 * _dot(act, wd_ref[...])
    if final:
        y = _rms_scale(y) * fg_ref[...]
    o_ref[...] = y


def _ffn(h, g, w_gu, wd, final_g=None):
    n = h.shape[0]
    tm = min(TOKEN_TILE, n)
    final = final_g is not None
    row = pl.BlockSpec((tm, D_MODEL), lambda i: (i, 0))
    vec = _resident((1, D_MODEL), lambda i: (0, 0))
    in_specs = [row, vec,
                _resident((D_MODEL, D_FF), lambda i: (0, 0)),
                _resident((D_MODEL, D_FF), lambda i: (0, 1)),
                _resident((D_FF, D_MODEL), lambda i: (0, 0))]
    args = [h, g.reshape(1, D_MODEL), w_gu, w_gu, wd]
    if final:
        in_specs.append(vec)
        args.append(final_g.reshape(1, D_MODEL))
    return pl.pallas_call(
        functools.partial(_ffn_kernel, final=final),
        grid=(n // tm,),
        in_specs=in_specs,
        out_specs=row,
        out_shape=jax.ShapeDtypeStruct((n, D_MODEL), _F32),
        compiler_params=_params(1),
        name="ffn_final" if final else "ffn",
    )(*args)


def _mix_in_kernel(h_ref, g_ref, cos_ref, sin_ref, w_ref, wvt_ref,
                   qd_ref, kd_ref, qn_ref, ks_ref, kw_ref, vt_ref,
                   craw_ref, gn_ref, ga_ref, gb_ref, craw_scr):
    x = h_ref[...]
    u = (_rms_scale(x) * g_ref[...]).astype(_BF16)
    cos = cos_ref[...]
    sin = sin_ref[...]
    lane = lax.broadcasted_iota(jnp.int32, (1, LANES), 1)
    first_half = (lane & (HEAD_DIM - 1)) < ROPE_HALF

    def proj(c0, width):
        return _dot(u, w_ref[:, c0:c0 + width])

    def rope_store(o_ref, c0, width, scale):
        z = proj(c0, width)
        for c in range(width // LANES):
            zc = z[:, c * LANES:(c + 1) * LANES]
            partner = jnp.where(first_half, pltpu.roll(zc, LANES - ROPE_HALF, axis=1),
                                pltpu.roll(zc, ROPE_HALF, axis=1))
            r = zc * cos + partner * sin
            if scale != 1.0:
                r = r * scale
            o_ref[:, c * LANES:(c + 1) * LANES] = r.astype(o_ref.dtype)

    rope_store(qd_ref, _C_QD, 512, ATTN_SCALE)
    rope_store(kd_ref, _C_KD, 512, 1.0)
    rope_store(qn_ref, _C_QN, 512, ATTN_SCALE)
    rope_store(ks_ref, _C_KS, 256, 1.0)
    rope_store(kw_ref, _C_KW, 256, 1.0)
    vt_ref[...] = _dot_nt(wvt_ref[...], u).astype(_BF16)
    craw = proj(_C_CRAW, 256)
    for kind in range(2):
        craw_scr[kind] = craw[:, kind * LANES:(kind + 1) * LANES]
        for t in range(CMP_STRIDE):
            every = pl.ds(t, craw_scr.shape[1] // CMP_STRIDE, stride=CMP_STRIDE)
            craw_ref[kind, :, t * LANES:(t + 1) * LANES] = craw_scr[kind, every, :]
    gn_ref[...] = _sigmoid(proj(_C_GN, 256))
    ga_ref[...] = _sigmoid(proj(_C_GA, D_MODEL)).astype(_BF16)
    gb_ref[...] = _sigmoid(proj(_C_GB, D_MODEL)).astype(_BF16)


def _mix_in(h, g, cos_t, sin_t, w, wvt):
    n = h.shape[0]
    tm = min(TOKEN_TILE, n)
    vd_width = wvt.shape[0]

    def row(width):
        return pl.BlockSpec((tm, width), lambda i: (i, 0))

    def bf16_rows(width):
        return row(width), jax.ShapeDtypeStruct((n, width), _BF16)

    outs = [bf16_rows(512), bf16_rows(512), bf16_rows(512), bf16_rows(256), bf16_rows(256),
            (pl.BlockSpec((vd_width, tm), lambda i: (0, i)), jax.ShapeDtypeStruct((vd_width, n), _BF16))]
    out_specs = [spec for spec, _ in outs]
    out_shape = [shape for _, shape in outs]
    assert tm % (CMP_STRIDE * 8) == 0
    out_specs += [pl.BlockSpec((2, tm // CMP_STRIDE, CMP_STRIDE * LANES), lambda i: (0, i, 0)),
                  row(256), row(D_MODEL), row(D_MODEL)]
    out_shape += [jax.ShapeDtypeStruct((2, n // CMP_STRIDE, CMP_STRIDE * LANES), _F32),
                  jax.ShapeDtypeStruct((n, 256), _F32),
                  jax.ShapeDtypeStruct((n, D_MODEL), _BF16),
                  jax.ShapeDtypeStruct((n, D_MODEL), _BF16)]
    return pl.pallas_call(
        _mix_in_kernel,
        grid=(n // tm,),
        in_specs=[row(D_MODEL), _resident((1, D_MODEL), lambda i: (0, 0)),
                  row(LANES), row(LANES),
                  _resident((D_MODEL, _C_TOTAL), lambda i: (0, 0)),
                  _resident((vd_width, D_MODEL), lambda i: (0, 0))],
        out_specs=out_specs,
        out_shape=out_shape,
        scratch_shapes=[pltpu.VMEM((2, tm, LANES), _F32)],
        compiler_params=_params(1),
        name="mix_in",
    )(h, g.reshape(1, D_MODEL), cos_t, sin_t, w, wvt)


def _compress_kernel(x_ref, pos_ref, w1_ref, w2_ref, w2t_ref, o_ref, ot_ref, *, n_cmp):
    x = x_ref[0, 0]
    rows = x.shape[0]
    x_top = (x + pos_ref[0, 0]).astype(_BF16)
    x_bot = (x + pos_ref[0, 1]).astype(_BF16)
    valid = lax.broadcasted_iota(jnp.int32, (rows, 1), 0) < n_cmp
    valid_t = lax.broadcasted_iota(jnp.int32, (1, rows), 1) < n_cmp
    for g in range(NSA_GROUPS):
        top = _dot(x_top, w1_ref[0, g, 0])
        bot = _dot(x_bot, w1_ref[0, g, 1])
        pre = top + pltpu.roll(bot, rows - 1, axis=0)
        hid = (pre * _sigmoid(pre)).astype(_BF16)
        out = _dot(hid, w2_ref[0])
        out = jnp.where(valid, out, 0.0)
        o_ref[0, 0, :, g * LANES:(g + 1) * LANES] = out.astype(o_ref.dtype)
        out_t = jnp.where(valid_t, _dot_nt(w2t_ref[0], hid), 0.0)
        ot_ref[0, 0, g * HEAD_DIM:(g + 1) * HEAD_DIM, :] = out_t.astype(ot_ref.dtype)


def _compress(craw, pos_x, w1_x, w2_x, w2_t, batch, seq):
    rows = seq // CMP_STRIDE
    n_cmp = (seq - CMP_BLOCK) // CMP_STRIDE + 1
    width = CMP_STRIDE * LANES
    x = craw.reshape(2, batch, rows, width)
    return pl.pallas_call(
        functools.partial(_compress_kernel, n_cmp=n_cmp),
        grid=(2, batch),
        in_specs=[pl.BlockSpec((1, 1, rows, width), lambda k, b: (k, b, 0, 0)),
                  pl.BlockSpec((1, 2, 1, width), lambda k, b: (k, 0, 0, 0)),
                  pl.BlockSpec((1, NSA_GROUPS, 2, width, CMP_HIDDEN), lambda k, b: (k, 0, 0, 0, 0)),
                  pl.BlockSpec((1, CMP_HIDDEN, LANES), lambda k, b: (k, 0, 0)),
                  pl.BlockSpec((1, HEAD_DIM, CMP_HIDDEN), lambda k, b: (k, 0, 0))],
        out_specs=[pl.BlockSpec((1, 1, rows, NSA_GROUPS * LANES), lambda k, b: (k, b, 0, 0)),
                   pl.BlockSpec((1, 1, NSA_GROUPS * HEAD_DIM, rows), lambda k, b: (k, b, 0, 0))],
        out_shape=[jax.ShapeDtypeStruct((2, batch, rows, NSA_GROUPS * LANES), _BF16),
                   jax.ShapeDtypeStruct((2, batch, NSA_GROUPS * HEAD_DIM, rows), _BF16)],
        compiler_params=_params(2),
        name="compress",
    )(x, pos_x, w1_x, w2_x, w2_t)


def _mask_bias(mask):
    return jnp.where(mask, 0.0, -jnp.inf).astype(_F32)


def _diff_attn_kernel(lam_ref, q_ref, k_ref, vt_ref, o_ref, *, lam_init):
    i = pl.program_id(1)
    tq = DIFF_TILE
    lane = lax.broadcasted_iota(jnp.int32, (1, LANES), 1)
    heads = range(DIFF_HEADS)

    qs = []
    for h in heads:
        q = q_ref[:, h * LANES:(h + 1) * LANES]
        zero = jnp.zeros_like(q)
        qs.append(jnp.concatenate([jnp.where(lane < HEAD_DIM, q, zero), jnp.where(lane >= HEAD_DIM, q, zero)], axis=0))

    def scores(h, keys):
        return _dot_nt(k_ref[keys, h * LANES:(h + 1) * LANES], qs[h])

    def weighted(h, p, keys):
        ones_rows = jnp.ones((ONES_ROWS, keys.size), _BF16)
        v_one = jnp.concatenate([vt_ref[h * LANES:(h + 1) * LANES, keys], ones_rows], axis=0)
        return _dot(v_one, p)

    def chunk(first_key, n_keys, carry, bias):
        keys = pl.ds(pl.multiple_of(first_key, tq), n_keys)
        s = [scores(h, keys) for h in heads]
        if bias is not None:
            s = [x + bias for x in s]
        m_new = [jnp.max(x, axis=0, keepdims=True) for x in s]
        if carry is not None:
            m_new = [jnp.maximum(carry[h][0], m_new[h]) for h in heads]
        p = [jnp.exp((s[h] - m_new[h]).astype(_BF16)) for h in heads]
        res = [weighted(h, p[h], keys) for h in heads]
        if carry is not None:
            res = [jnp.exp(carry[h][0] - m_new[h]) * carry[h][1] + res[h] for h in heads]
        return tuple((m_new[h], res[h]) for h in heads)

    key = lax.broadcasted_iota(jnp.int32, (tq, tq), 0)
    qry = lax.broadcasted_iota(jnp.int32, (tq, tq), 1)
    tri = _mask_bias(key <= qry)
    tri = jnp.concatenate([tri, tri], axis=1)
    carry = lax.cond(i % 2 == 1,
                     lambda: chunk((i - 1) * tq, 2 * tq, None, jnp.concatenate([jnp.zeros_like(tri), tri], axis=0)),
                     lambda: chunk(i * tq, tq, None, tri))
    carry = lax.fori_loop(0, i // 2, lambda j, c: chunk(j * (2 * tq), 2 * tq, c, None), carry)
    lp = lam_ref[...]
    lam = (jnp.exp(jnp.sum(lp[0:1] * lp[1:2], axis=1, keepdims=True))
           - jnp.exp(jnp.sum(lp[2:3] * lp[3:4], axis=1, keepdims=True)) + lam_init)
    for h in heads:
        _, acc = carry[h]
        o = acc[:LANES] * (1.0 / acc[LANES:LANES + 1])
        o = o[:, :tq] - lam * o[:, tq:]
        o = o * lax.rsqrt(jnp.mean(o * o, axis=0, keepdims=True) + EPS) * (1.0 - lam_init)
        o_ref[:, h * LANES:(h + 1) * LANES] = o.T.astype(o_ref.dtype)


def _diff_attn(lam_p, qd, kd, vdt, batch, seq, lam_init):
    tq = DIFF_TILE
    nq = seq // tq
    n = batch * seq
    width = DIFF_HEADS * LANES
    return pl.pallas_call(
        functools.partial(_diff_attn_kernel, lam_init=lam_init),
        grid=(batch, nq),
        in_specs=[pl.BlockSpec((4, HEAD_DIM), lambda b, i: (0, 0)),
                  pl.BlockSpec((tq, width), lambda b, i: (b * nq + i, 0)),
                  pl.BlockSpec((seq, width), lambda b, i: (b, 0)),
                  pl.BlockSpec((width, seq), lambda b, i: (0, b))],
        out_specs=pl.BlockSpec((tq, width), lambda b, i: (b * nq + i, 0)),
        out_shape=jax.ShapeDtypeStruct((n, width), _BF16),
        compiler_params=_params(2),
        name="diff_attn",
    )(lam_p, qd, kd, vdt)


def _nsa_kernel(q_ref, gate_ref, ovt_ref, blk_ref, kc_ref, vct_ref, ks_ref, kw_ref, vt_ref, o_ref):
    i = pl.program_id(1)
    tq = NSA_TILE
    hpg = NSA_HPG
    groups = range(NSA_GROUPS)
    lane = lax.broadcasted_iota(jnp.int32, (1, LANES), 1)
    low = lane < HEAD_DIM
    q0 = i * tq

    def per_head(x):
        return jnp.concatenate([x] * hpg, axis=0)

    def group_cols(ref, rows, g):
        return ref[rows, g * LANES:(g + 1) * LANES]

    def stack_q(g):
        parts = []
        for j in range(hpg):
            c0 = g * hpg * HEAD_DIM + (j // 2) * LANES
            qc = q_ref[:, c0:c0 + LANES]
            keep = low if j % 2 == 0 else jnp.logical_not(low)
            parts.append(jnp.where(keep, qc, jnp.zeros_like(qc)))
        return jnp.concatenate(parts, axis=0)

    def per_head_cols(x):
        return jnp.concatenate([x] * hpg, axis=1)

    qpos_row = q0 + lax.broadcasted_iota(jnp.int32, (1, tq), 1)

    def compressed_and_select(g, qs):
        kc = kc_ref[0, 0, :, g * LANES:(g + 1) * LANES]
        vc_t = vct_ref[0, 0, g * HEAD_DIM:(g + 1) * HEAD_DIM, :]
        ncp = kc.shape[0]
        s_c = _dot_nt(kc, qs)
        cmp_end = lax.broadcasted_iota(jnp.int32, (ncp, 1), 0) * CMP_STRIDE + (CMP_BLOCK - 1)
        s_c = s_c + per_head_cols(_mask_bias(cmp_end <= qpos_row))
        mx = jnp.max(s_c, axis=0, keepdims=True)
        mx = jnp.where(mx == -jnp.inf, 0.0, mx)
        e = jnp.exp((s_c - mx).astype(_BF16))
        ones_rows = jnp.ones((ONES_ROWS, ncp), _BF16)
        res = _dot(jnp.concatenate([vc_t, ones_rows], axis=0), e)
        inv = 1.0 / jnp.maximum(res[HEAD_DIM:HEAD_DIM + 1], 1e-30)
        o_c = res[:HEAD_DIM] * inv

        ov_t = ovt_ref[...]
        n_blk = ov_t.shape[0]
        per_head_imp = _dot(ov_t, e) * inv
        imp = per_head_imp[:, 0:tq]
        for j in range(1, hpg):
            imp = imp + per_head_imp[:, j * tq:(j + 1) * tq]

        jsel = lax.broadcasted_iota(jnp.int32, (n_blk, 1), 0)
        cur = qpos_row >> SEL_SHIFT
        forced = (jsel == 0) | (jsel == cur) | (jsel == cur - 1)
        future = jsel * SEL_BLOCK > qpos_row
        work = jnp.where(future, -1.0, jnp.where(forced, FORCED_SCORE, imp))

        jsel_f = jsel.astype(_F32)
        chosen = jnp.zeros((n_blk, tq), _F32)
        for _ in range(SEL_TOPK):
            best = jnp.max(work, axis=0, keepdims=True)
            first = jnp.min(jnp.where(work == best, jsel_f, float(SEL_PAD)), axis=0, keepdims=True)
            pick = jsel_f == first
            chosen = jnp.where(pick, 1.0, chosen)
            work = jnp.where(pick, -jnp.inf, work)
        if n_blk < SEL_PAD:
            chosen = jnp.concatenate([chosen, jnp.ones((SEL_PAD - n_blk, tq), _F32)], axis=0)
        return o_c, chosen

    qs, o_c, q_sel = [], [], []
    for g in groups:
        qs.append(stack_q(g))
        oc, chosen = compressed_and_select(g, qs[g])
        o_c.append(oc)
        penalty = ((chosen - 1.0) * MASKED_SCORE).T.astype(_BF16)
        q_sel.append(jnp.concatenate([qs[g], per_head(penalty)], axis=1))


    def v_rows(kind, g, keys):
        r0 = (kind * NSA_GROUPS + g) * HEAD_DIM
        ones_rows = jnp.ones((ONES_ROWS, keys.size), _BF16)
        return jnp.concatenate([vt_ref[r0:r0 + HEAD_DIM, keys], ones_rows], axis=0)

    def sel_scores(g, keys):
        k_blk = jnp.concatenate([group_cols(ks_ref, keys, g), blk_ref[keys, :]], axis=1)
        return _dot_nt(k_blk, q_sel[g])

    def sel_chunk(first_key, n_keys, carry, bias):
        keys = pl.ds(pl.multiple_of(first_key, tq), n_keys)
        s = [sel_scores(g, keys) for g in groups]
        if bias is not None:
            s = [x + bias for x in s]
        m_new = [jnp.max(x, axis=0, keepdims=True) for x in s]
        if carry is not None:
            m_new = [jnp.maximum(carry[g][0], m_new[g]) for g in groups]
        p = [jnp.exp((s[g] - m_new[g]).astype(_BF16)) for g in groups]
        res = [_dot(v_rows(0, g, keys), p[g]) for g in groups]
        if carry is not None:
            res = [jnp.exp(carry[g][0] - m_new[g]) * carry[g][1] + res[g] for g in groups]
        return tuple((m_new[g], res[g]) for g in groups)

    sub = tq // NSA_WINDOW_SPLIT
    wlen = WINDOW + sub
    key_w = lax.broadcasted_iota(jnp.int32, (wlen, sub), 0)
    qry_w = lax.broadcasted_iota(jnp.int32, (wlen, sub), 1)
    res_sub = []
    for t in range(NSA_WINDOW_SPLIT):
        w0 = jnp.maximum(q0 + t * sub - WINDOW, 0)
        wkeys = pl.ds(pl.multiple_of(w0, sub), wlen)
        qry = qry_w + (q0 + t * sub - w0)
        band = per_head_cols(
            jnp.where(key_w <= qry, jnp.where(key_w > qry - WINDOW, 0.0, -jnp.inf), -jnp.inf).astype(_F32))
        q_sub = [jnp.concatenate([qs[g][j * tq + t * sub:j * tq + (t + 1) * sub] for j in range(hpg)], axis=0)
                 for g in groups]
        s_w = [_dot_nt(group_cols(kw_ref, wkeys, g), q_sub[g]) + band for g in groups]
        p_w = [jnp.exp((x - jnp.max(x, axis=0, keepdims=True)).astype(_BF16)) for x in s_w]
        res_sub.append([_dot(v_rows(1, g, wkeys), p_w[g]) for g in groups])
    res_w = [jnp.concatenate([res_sub[t][g][:, j * sub:(j + 1) * sub]
                              for j in range(hpg) for t in range(NSA_WINDOW_SPLIT)], axis=1) for g in groups]

    key_t = lax.broadcasted_iota(jnp.int32, (tq, tq), 0)
    qry_t = lax.broadcasted_iota(jnp.int32, (tq, tq), 1)
    tri = per_head_cols(_mask_bias(key_t <= qry_t))
    carry = lax.cond(i % 2 == 1,
                     lambda: sel_chunk(q0 - tq, 2 * tq, None, jnp.concatenate([jnp.zeros_like(tri), tri], axis=0)),
                     lambda: sel_chunk(q0, tq, None, tri))
    carry = lax.fori_loop(0, i // 2, lambda j, c: sel_chunk(j * (2 * tq), 2 * tq, c, None), carry)

    def normalised_rows(res):
        return res[:HEAD_DIM] * (1.0 / res[HEAD_DIM:HEAD_DIM + 1])

    for g in groups:
        gate_t = gate_ref[:, g * LANES:(g + 1) * LANES].T
        o_s = normalised_rows(carry[g][1])
        o_w = normalised_rows(res_w[g])
        mixed = []
        for j in range(hpg):
            cols = slice(j * tq, (j + 1) * tq)
            mixed.append(gate_t[3 * j:3 * j + 1] * o_c[g][:, cols] + gate_t[3 * j + 1:3 * j + 2] * o_s[:, cols]
                         + gate_t[3 * j + 2:3 * j + 3] * o_w[:, cols])
        for c in range(hpg // 2):
            pair = jnp.concatenate([mixed[2 * c], mixed[2 * c + 1]], axis=0)
            col = (g * (hpg // 2) + c) * LANES
            o_ref[:, col:col + LANES] = pair.T.astype(o_ref.dtype)


def _overlap_matrix(seq):
    rows = seq // CMP_STRIDE
    n_cmp = (seq - CMP_BLOCK) // CMP_STRIDE + 1
    n_sel = seq // SEL_BLOCK
    cs = np.arange(rows)[:, None] * CMP_STRIDE
    ss = np.arange(SEL_PAD)[None, :] * SEL_BLOCK
    ov = np.clip(np.minimum(cs + CMP_BLOCK, ss + SEL_BLOCK) - np.maximum(cs, ss), 0, None) / CMP_BLOCK
    ov = ov * (np.arange(rows)[:, None] < n_cmp) * (np.arange(SEL_PAD)[None, :] < n_sel)
    n_blk = _sel_rows(seq)
    return jnp.asarray(ov.T[:n_blk], dtype=_BF16)


def _sel_rows(seq):
    n_sel = seq // SEL_BLOCK
    assert SEL_TOPK <= n_sel <= SEL_PAD
    return min(SEL_PAD, -(-n_sel // BF16_SUBLANES) * BF16_SUBLANES)


def _block_onehot(seq):
    hit = (np.arange(seq)[:, None] >> SEL_SHIFT) == np.arange(SEL_PAD)[None, :]
    return jnp.asarray(hit, dtype=_BF16)


def _nsa(qn, gn, ckv, ckv_t, ks, kw, vt, batch, seq):
    tq = NSA_TILE
    nq = seq // tq
    n = batch * seq
    rows = seq // CMP_STRIDE
    width = NSA_GROUPS * LANES
    assert seq // SEL_BLOCK <= SEL_PAD and seq >= WINDOW + tq
    assert tq % NSA_WINDOW_SPLIT == 0 and WINDOW % (tq // NSA_WINDOW_SPLIT) == 0
    assert _VT_DIFF_ROWS % _VT_NSA_ROWS == 0
    qmap = lambda b, i: (b * nq + i, 0)
    kvmap = lambda b, i: (b, 0)
    return pl.pallas_call(
        _nsa_kernel,
        grid=(batch, nq),
        in_specs=[pl.BlockSpec((tq, NSA_HEADS * HEAD_DIM), qmap),
                  pl.BlockSpec((tq, width), qmap),
                  _resident((_sel_rows(seq), rows), lambda b, i: (0, 0)),
                  _resident((seq, SEL_PAD), lambda b, i: (0, 0)),
                  pl.BlockSpec((1, 1, rows, width), lambda b, i: (0, b, 0, 0)),
                  pl.BlockSpec((1, 1, NSA_GROUPS * HEAD_DIM, rows), lambda b, i: (1, b, 0, 0)),
                  pl.BlockSpec((seq, width), kvmap), pl.BlockSpec((seq, width), kvmap),
                  pl.BlockSpec((_VT_NSA_ROWS, seq), lambda b, i: (_VT_DIFF_ROWS // _VT_NSA_ROWS, b))],
        out_specs=pl.BlockSpec((tq, NSA_HEADS * HEAD_DIM), qmap),
        out_shape=jax.ShapeDtypeStruct((n, NSA_HEADS * HEAD_DIM), _BF16),
        compiler_params=_params(2),
        name="nsa",
    )(qn, gn, _overlap_matrix(seq), _block_onehot(seq), ckv, ckv_t, ks, kw, vt)


def _merge_kernel(h_ref, oa_ref, ob_ref, ga_ref, gb_ref, wa_ref, wb_ref, wo_ref, o_ref):
    y = (ga_ref[...].astype(_F32) * _dot(oa_ref[...], wa_ref[...])
         + gb_ref[...].astype(_F32) * _dot(ob_ref[...], wb_ref[...]))
    o_ref[...] = h_ref[...] + _dot(y.astype(_BF16), wo_ref[...])


def _merge(h, oa, ob, ga, gb, wa, wb, wo):
    n = h.shape[0]
    tm = min(TOKEN_TILE, n)

    def row(width):
        return pl.BlockSpec((tm, width), lambda i: (i, 0))

    return pl.pallas_call(
        _merge_kernel,
        grid=(n // tm,),
        in_specs=[row(D_MODEL), row(512), row(512), row(D_MODEL), row(D_MODEL),
                  _resident((512, D_MODEL), lambda i: (0, 0)),
                  _resident((512, D_MODEL), lambda i: (0, 0)),
                  _resident((D_MODEL, D_MODEL), lambda i: (0, 0))],
        out_specs=row(D_MODEL),
        out_shape=jax.ShapeDtypeStruct((n, D_MODEL), _F32),
        compiler_params=_params(1),
        name="merge",
    )(h, oa, ob, ga, gb, wa, wb, wo)


def _dup_groups(w):
    a, b = w[:, :HEAD_DIM], w[:, HEAD_DIM:]
    return jnp.concatenate([a, a, b, b], axis=1)


def _prep_w_in(w):
    col = lambda k: w[:, _OFFS[k]:_OFFS[k + 1]]
    q_d, k_d, v_d, q_n, kc, vc, ks, vs, kw, vw, g_n, g_a, g_b = [col(k) for k in range(13)]
    per_group = NSA_HPG * 3
    pad = jnp.zeros((w.shape[0], LANES - per_group), w.dtype)
    gn = jnp.concatenate([g_n[:, :per_group], pad, g_n[:, per_group:], pad], axis=1)
    out = jnp.concatenate([q_d, k_d, q_n, _dup_groups(ks), _dup_groups(kw), kc, vc, gn, g_a, g_b], axis=1)
    assert out.shape[1] == _C_TOTAL
    w_vt = jnp.concatenate([v_d, vs, vw], axis=1).T
    assert w_vt.shape[0] == _VT_DIFF_ROWS + _VT_NSA_ROWS
    return out.astype(_BF16), w_vt.astype(_BF16)


def _prep_compress(pos, w1, w2):
    half = CMP_BLOCK // 2
    w1 = w1.reshape(2, 2, half, HEAD_DIM, CMP_HIDDEN)
    zeros = jnp.zeros_like(w1)
    per_group = []
    for g in range(NSA_GROUPS):
        lanes = [zeros, zeros]
        lanes[g] = w1
        per_group.append(jnp.concatenate(lanes, axis=3))
    w1_x = jnp.stack(per_group, axis=1).reshape(2, NSA_GROUPS, 2, half * LANES, CMP_HIDDEN).astype(_BF16)
    pos = pos.reshape(2, 2, half, HEAD_DIM)
    pos_x = jnp.concatenate([pos, pos], axis=3).reshape(2, 2, 1, half * LANES).astype(_F32)
    w2_x = jnp.concatenate([w2, w2], axis=2).astype(_BF16)
    w2_t = jnp.swapaxes(w2, 1, 2).astype(_BF16)
    return pos_x, w1_x, w2_x, w2_t


def kernel(x, positions, ffn1_norm, ffn1_w_gu, ffn1_w_down, mix_norm, w_in, diff_lambda, cmp_pos, cmp_w1, cmp_w2, w_branch_a, w_branch_b, w_out, ffn2_norm, ffn2_w_gu, ffn2_w_down, final_norm):
    batch, seq, _ = x.shape
    n = batch * seq
    depth = w_in.shape[0]
    assert seq % max(NSA_TILE, DIFF_TILE, CMP_STRIDE * 8) == 0
    cos_t, sin_t = _rope_tables(positions)
    h = x.reshape(n, D_MODEL)
    for l in range(depth):
        bf = lambda a: a.astype(_BF16)
        h = _ffn(h, ffn1_norm[l], bf(ffn1_w_gu[l]), bf(ffn1_w_down[l]))
        (qd, kd, qn, ks, kw, vt, craw, gn, ga, gb) = _mix_in(h, mix_norm[l], cos_t, sin_t, *_prep_w_in(w_in[l]))
        ckv, ckv_t = _compress(craw, *_prep_compress(cmp_pos[l], cmp_w1[l], cmp_w2[l]), batch, seq)
        lam_init = 0.8 - 0.6 * math.exp(-0.3 * l)
        o_a = _diff_attn(diff_lambda[l].astype(_F32), qd, kd, vt, batch, seq, lam_init)
        o_b = _nsa(qn, gn, ckv, ckv_t, ks, kw, vt, batch, seq)
        h = _merge(h, o_a, o_b, ga, gb, bf(w_branch_a[l]), bf(w_branch_b[l]), bf(w_out[l]))
        last = l == depth - 1
        h = _ffn(h, ffn2_norm[l], bf(ffn2_w_gu[l]), bf(ffn2_w_down[l]), final_g=final_norm if last else None)
    return h.reshape(batch, seq, D_MODEL)
```

```python
import functools
import math

import numpy as np
import jax
import jax.numpy as jnp
from jax import lax
from jax.experimental import pallas as pl
from jax.experimental.pallas import tpu as pltpu

D_MODEL = 1024
D_FF = 2816
HEAD_DIM = 64
ROPE_DIM = HEAD_DIM // 4
ROPE_HALF = ROPE_DIM // 2
ROPE_THETA = 500000.0
EPS = 1e-6
DIFF_HEADS = 4
NSA_HEADS = 8
NSA_GROUPS = 2
NSA_HPG = NSA_HEADS // NSA_GROUPS
CMP_BLOCK = 32
CMP_STRIDE = 16
CMP_HIDDEN = 256
SEL_BLOCK = 64
SEL_SHIFT = SEL_BLOCK.bit_length() - 1
SEL_TOPK = 8
WINDOW = 512
FORCED_SCORE = 1.0e4
MASKED_SCORE = 2.0 ** 100
ATTN_SCALE = HEAD_DIM ** -0.5

_SPLITS = (512, 512, 512, 512, 128, 128, 128, 128, 128, 128, 24, 1024, 1024)
_OFFS = np.concatenate([[0], np.cumsum(_SPLITS)]).tolist()

LANES = 128
VMEM_LIMIT_BYTES = 56 * 1024 * 1024
TOKEN_TILE = 512
DIFF_TILE = 256
NSA_TILE = 256
NSA_WINDOW_SPLIT = 2
SEL_PAD = 128

_C_QD, _C_KD, _C_QN, _C_KS, _C_KW = 0, 512, 1024, 1536, 1792
_C_ROPE_END = 2048
_C_CRAW, _C_GN, _C_GA, _C_GB = 2048, 2304, 2560, 3584
_C_TOTAL = 4608
_VT_DIFF_ROWS = 512
_VT_NSA_ROWS = 2 * NSA_GROUPS * HEAD_DIM
BF16_SUBLANES = 16
ONES_ROWS = BF16_SUBLANES

_BF16 = jnp.bfloat16
_F32 = jnp.float32


def _params(n_grid):
    return pltpu.CompilerParams(
        dimension_semantics=("arbitrary",) * n_grid,
        vmem_limit_bytes=VMEM_LIMIT_BYTES,
    )


def _resident(shape, index_map):
    return pl.BlockSpec(shape, index_map, pipeline_mode=pl.Buffered(1))


def _dot(a, b):
    return jnp.dot(a, b, preferred_element_type=_F32)


def _dot_nt(a, b):
    return lax.dot_general(a, b, (((1,), (1,)), ((), ())), preferred_element_type=_F32)


def _sigmoid(x):
    return 1.0 / (1.0 + jnp.exp(-x))


def _rms_scale(x):
    return x * lax.rsqrt(jnp.mean(x * x, axis=-1, keepdims=True) + EPS)


def _rope_table_kernel(pos_ref, cos_ref, sin_ref):
    lane = lax.broadcasted_iota(jnp.int32, (1, LANES), 1)
    in_head = lane & (HEAD_DIM - 1)
    freq = (in_head & (ROPE_HALF - 1)).astype(_F32)
    inv = jnp.exp(freq * (-2.0 / ROPE_DIM * math.log(ROPE_THETA)))
    inv = jnp.where(in_head < ROPE_DIM, inv, 0.0)
    ang = pos_ref[...].astype(_F32) * inv
    cos_ref[...] = jnp.cos(ang)
    sin = jnp.sin(ang)
    sin_ref[...] = jnp.where(in_head < ROPE_HALF, -sin, sin)


def _rope_tables(positions):
    n = positions.size
    tm = min(TOKEN_TILE, n)
    pos = positions.reshape(n, 1)
    return pl.pallas_call(
        _rope_table_kernel,
        grid=(n // tm,),
        in_specs=[pl.BlockSpec((tm, 1), lambda i: (i, 0))],
        out_specs=[pl.BlockSpec((tm, LANES), lambda i: (i, 0))] * 2,
        out_shape=[jax.ShapeDtypeStruct((n, LANES), _F32)] * 2,
        compiler_params=_params(1),
        name="rope_tables",
    )(pos)


def _ffn_kernel(h_ref, *rest, final, merge):
    rest = list(rest)
    x = h_ref[...]
    if merge:
        oa_ref, ob_ref, ga_ref, gb_ref, wa_ref, wb_ref, wo_ref = rest[:7]
        rest = rest[7:]
        y = (ga_ref[...].astype(_F32) * _dot(oa_ref[...], wa_ref[...])
             + gb_ref[...].astype(_F32) * _dot(ob_ref[...], wb_ref[...]))
        x = x + _dot(y.astype(_BF16), wo_ref[...])
    g_ref, wg_ref, wu_ref, wd_ref = rest[:4]
    if final:
        fg_ref, o_ref = rest[4:]
    else:
        (o_ref,) = rest[4:]
    u = (_rms_scale(x) * g_ref[...]).astype(_BF16)
    gate = _dot(u, wg_ref[...])
    up = _dot(u, wu_ref[...])
    act = (gate * _sigmoid(gate) * up).astype(_BF16)
    y = x + 0.5 * _dot(act, wd_ref[...])
    if final:
        y = _rms_scale(y) * fg_ref[...]
    o_ref[...] = y


def _ffn(h, g, w_gu, wd, final_g=None, mixers=None):
    n = h.shape[0]
    tm = min(TOKEN_TILE, n)
    final = final_g is not None
    merge = mixers is not None

    def rows(width):
        return pl.BlockSpec((tm, width), lambda i: (i, 0))

    row = rows(D_MODEL)
    vec = _resident((1, D_MODEL), lambda i: (0, 0))
    in_specs, args = [row], [h]
    if merge:
        in_specs += [rows(512), rows(512), row, row,
                     _resident((512, D_MODEL), lambda i: (0, 0)),
                     _resident((512, D_MODEL), lambda i: (0, 0)),
                     _resident((D_MODEL, D_MODEL), lambda i: (0, 0))]
        args += list(mixers)
    in_specs += [vec,
                 _resident((D_MODEL, D_FF), lambda i: (0, 0)),
                 _resident((D_MODEL, D_FF), lambda i: (0, 1)),
                 _resident((D_FF, D_MODEL), lambda i: (0, 0))]
    args += [g.reshape(1, D_MODEL), w_gu, w_gu, wd]
    if final:
        in_specs.append(vec)
        args.append(final_g.reshape(1, D_MODEL))
    return pl.pallas_call(
        functools.partial(_ffn_kernel, final=final, merge=merge),
        grid=(n // tm,),
        in_specs=in_specs,
        out_specs=row,
        out_shape=jax.ShapeDtypeStruct((n, D_MODEL), _F32),
        compiler_params=_params(1),
        name=("merge_" if merge else "") + ("ffn_final" if final else "ffn"),
    )(*args)


def _mix_in_kernel(h_ref, g_ref, cos_ref, sin_ref, w_ref, wvt_ref,
                   qd_ref, kd_ref, qn_ref, ks_ref, kw_ref, vt_ref,
                   craw_ref, gn_ref, ga_ref, gb_ref, craw_scr):
    x = h_ref[...]
    u = (_rms_scale(x) * g_ref[...]).astype(_BF16)
    cos = cos_ref[...]
    sin = sin_ref[...]
    lane = lax.broadcasted_iota(jnp.int32, (1, LANES), 1)
    first_half = (lane & (HEAD_DIM - 1)) < ROPE_HALF

    def proj(c0, width):
        return _dot(u, w_ref[:, c0:c0 + width])

    def rope_store(o_ref, c0, width, scale):
        z = proj(c0, width)
        for c in range(width // LANES):
            zc = z[:, c * LANES:(c + 1) * LANES]
            partner = jnp.where(first_half, pltpu.roll(zc, LANES - ROPE_HALF, axis=1),
                                pltpu.roll(zc, ROPE_HALF, axis=1))
            r = zc * cos + partner * sin
            if scale != 1.0:
                r = r * scale
            o_ref[:, c * LANES:(c + 1) * LANES] = r.astype(o_ref.dtype)

    rope_store(qd_ref, _C_QD, 512, ATTN_SCALE)
    rope_store(kd_ref, _C_KD, 512, 1.0)
    rope_store(qn_ref, _C_QN, 512, ATTN_SCALE)
    rope_store(ks_ref, _C_KS, 256, 1.0)
    rope_store(kw_ref, _C_KW, 256, 1.0)
    vt_ref[...] = _dot_nt(wvt_ref[...], u).astype(_BF16)
    craw = proj(_C_CRAW, 256)
    for kind in range(2):
        craw_scr[kind] = craw[:, kind * LANES:(kind + 1) * LANES]
        for t in range(CMP_STRIDE):
            every = pl.ds(t, craw_scr.shape[1] // CMP_STRIDE, stride=CMP_STRIDE)
            craw_ref[kind, :, t * LANES:(t + 1) * LANES] = craw_scr[kind, every, :]
    gn_ref[...] = _sigmoid(proj(_C_GN, 256))
    ga_ref[...] = _sigmoid(proj(_C_GA, D_MODEL)).astype(_BF16)
    gb_ref[...] = _sigmoid(proj(_C_GB, D_MODEL)).astype(_BF16)


def _mix_in(h, g, cos_t, sin_t, w, wvt):
    n = h.shape[0]
    tm = min(TOKEN_TILE, n)
    vd_width = wvt.shape[0]

    def row(width):
        return pl.BlockSpec((tm, width), lambda i: (i, 0))

    def bf16_rows(width):
        return row(width), jax.ShapeDtypeStruct((n, width), _BF16)

    outs = [bf16_rows(512), bf16_rows(512), bf16_rows(512), bf16_rows(256), bf16_rows(256),
            (pl.BlockSpec((vd_width, tm), lambda i: (0, i)), jax.ShapeDtypeStruct((vd_width, n), _BF16))]
    out_specs = [spec for spec, _ in outs]
    out_shape = [shape for _, shape in outs]
    assert tm % (CMP_STRIDE * 8) == 0
    out_specs += [pl.BlockSpec((2, tm // CMP_STRIDE, CMP_STRIDE * LANES), lambda i: (0, i, 0)),
                  row(256), row(D_MODEL), row(D_MODEL)]
    out_shape += [jax.ShapeDtypeStruct((2, n // CMP_STRIDE, CMP_STRIDE * LANES), _F32),
                  jax.ShapeDtypeStruct((n, 256), _F32),
                  jax.ShapeDtypeStruct((n, D_MODEL), _BF16),
                  jax.ShapeDtypeStruct((n, D_MODEL), _BF16)]
    return pl.pallas_call(
        _mix_in_kernel,
        grid=(n // tm,),
        in_specs=[row(D_MODEL), _resident((1, D_MODEL), lambda i: (0, 0)),
                  row(LANES), row(LANES),
                  _resident((D_MODEL, _C_TOTAL), lambda i: (0, 0)),
                  _resident((vd_width, D_MODEL), lambda i: (0, 0))],
        out_specs=out_specs,
        out_shape=out_shape,
        scratch_shapes=[pltpu.VMEM((2, tm, LANES), _F32)],
        compiler_params=_params(1),
        name="mix_in",
    )(h, g.reshape(1, D_MODEL), cos_t, sin_t, w, wvt)


def _compress_kernel(x_ref, pos_ref, w1_ref, w2_ref, w2t_ref, o_ref, ot_ref, *, n_cmp):
    x = x_ref[0, 0]
    rows = x.shape[0]
    x_top = (x + pos_ref[0, 0]).astype(_BF16)
    x_bot = (x + pos_ref[0, 1]).astype(_BF16)
    valid = lax.broadcasted_iota(jnp.int32, (rows, 1), 0) < n_cmp
    valid_t = lax.broadcasted_iota(jnp.int32, (1, rows), 1) < n_cmp
    for g in range(NSA_GROUPS):
        top = _dot(x_top, w1_ref[0, g, 0])
        bot = _dot(x_bot, w1_ref[0, g, 1])
        pre = top + pltpu.roll(bot, rows - 1, axis=0)
        hid = (pre * _sigmoid(pre)).astype(_BF16)
        out = _dot(hid, w2_ref[0])
        out = jnp.where(valid, out, 0.0)
        o_ref[0, 0, :, g * LANES:(g + 1) * LANES] = out.astype(o_ref.dtype)
        out_t = jnp.where(valid_t, _dot_nt(w2t_ref[0], hid), 0.0)
        ot_ref[0, 0, g * HEAD_DIM:(g + 1) * HEAD_DIM, :] = out_t.astype(ot_ref.dtype)


def _compress(craw, pos_x, w1_x, w2_x, w2_t, batch, seq):
    rows = seq // CMP_STRIDE
    n_cmp = (seq - CMP_BLOCK) // CMP_STRIDE + 1
    width = CMP_STRIDE * LANES
    x = craw.reshape(2, batch, rows, width)
    return pl.pallas_call(
        functools.partial(_compress_kernel, n_cmp=n_cmp),
        grid=(2, batch),
        in_specs=[pl.BlockSpec((1, 1, rows, width), lambda k, b: (k, b, 0, 0)),
                  pl.BlockSpec((1, 2, 1, width), lambda k, b: (k, 0, 0, 0)),
                  pl.BlockSpec((1, NSA_GROUPS, 2, width, CMP_HIDDEN), lambda k, b: (k, 0, 0, 0, 0)),
                  pl.BlockSpec((1, CMP_HIDDEN, LANES), lambda k, b: (k, 0, 0)),
                  pl.BlockSpec((1, HEAD_DIM, CMP_HIDDEN), lambda k, b: (k, 0, 0))],
        out_specs=[pl.BlockSpec((1, 1, rows, NSA_GROUPS * LANES), lambda k, b: (k, b, 0, 0)),
                   pl.BlockSpec((1, 1, NSA_GROUPS * HEAD_DIM, rows), lambda k, b: (k, b, 0, 0))],
        out_shape=[jax.ShapeDtypeStruct((2, batch, rows, NSA_GROUPS * LANES), _BF16),
                   jax.ShapeDtypeStruct((2, batch, NSA_GROUPS * HEAD_DIM, rows), _BF16)],
        compiler_params=_params(2),
        name="compress",
    )(x, pos_x, w1_x, w2_x, w2_t)


def _mask_bias(mask):
    return jnp.where(mask, 0.0, -jnp.inf).astype(_F32)


def _diff_attn_kernel(lam_ref, q_ref, k_ref, vt_ref, o_ref, *, lam_init):
    i = pl.program_id(1)
    tq = DIFF_TILE
    lane = lax.broadcasted_iota(jnp.int32, (1, LANES), 1)
    heads = range(DIFF_HEADS)

    qs = []
    for h in heads:
        q = q_ref[:, h * LANES:(h + 1) * LANES]
        zero = jnp.zeros_like(q)
        qs.append(jnp.concatenate([jnp.where(lane < HEAD_DIM, q, zero), jnp.where(lane >= HEAD_DIM, q, zero)], axis=0))

    def scores(h, keys):
        return _dot_nt(k_ref[keys, h * LANES:(h + 1) * LANES], qs[h])

    def weighted(h, p, keys):
        ones_rows = jnp.ones((ONES_ROWS, keys.size), _BF16)
        v_one = jnp.concatenate([vt_ref[h * LANES:(h + 1) * LANES, keys], ones_rows], axis=0)
        return _dot(v_one, p)

    def chunk(first_key, n_keys, carry, bias):
        keys = pl.ds(pl.multiple_of(first_key, tq), n_keys)
        s = [scores(h, keys) for h in heads]
        if bias is not None:
            s = [x + bias for x in s]
        m_new = [jnp.max(x, axis=0, keepdims=True) for x in s]
        if carry is not None:
            m_new = [jnp.maximum(carry[h][0], m_new[h]) for h in heads]
        p = [jnp.exp((s[h] - m_new[h]).astype(_BF16)) for h in heads]
        res = [weighted(h, p[h], keys) for h in heads]
        if carry is not None:
            res = [jnp.exp(carry[h][0] - m_new[h]) * carry[h][1] + res[h] for h in heads]
        return tuple((m_new[h], res[h]) for h in heads)

    key = lax.broadcasted_iota(jnp.int32, (tq, tq), 0)
    qry = lax.broadcasted_iota(jnp.int32, (tq, tq), 1)
    tri = _mask_bias(key <= qry)
    tri = jnp.concatenate([tri, tri], axis=1)
    carry = lax.cond(i % 2 == 1,
                     lambda: chunk((i - 1) * tq, 2 * tq, None, jnp.concatenate([jnp.zeros_like(tri), tri], axis=0)),
                     lambda: chunk(i * tq, tq, None, tri))
    carry = lax.fori_loop(0, i // 2, lambda j, c: chunk(j * (2 * tq), 2 * tq, c, None), carry)
    lp = lam_ref[...]
    lam = (jnp.exp(jnp.sum(lp[0:1] * lp[1:2], axis=1, keepdims=True))
           - jnp.exp(jnp.sum(lp[2:3] * lp[3:4], axis=1, keepdims=True)) + lam_init)
    for h in heads:
        _, acc = carry[h]
        o = acc[:LANES] * (1.0 / acc[LANES:LANES + 1])
        o = o[:, :tq] - lam * o[:, tq:]
        o = o * lax.rsqrt(jnp.mean(o * o, axis=0, keepdims=True) + EPS) * (1.0 - lam_init)
        o_ref[:, h * LANES:(h + 1) * LANES] = o.T.astype(o_ref.dtype)


def _diff_attn(lam_p, qd, kd, vdt, batch, seq, lam_init):
    tq = DIFF_TILE
    nq = seq // tq
    n = batch * seq
    width = DIFF_HEADS * LANES
    return pl.pallas_call(
        functools.partial(_diff_attn_kernel, lam_init=lam_init),
        grid=(batch, nq),
        in_specs=[pl.BlockSpec((4, HEAD_DIM), lambda b, i: (0, 0)),
                  pl.BlockSpec((tq, width), lambda b, i: (b * nq + i, 0)),
                  pl.BlockSpec((seq, width), lambda b, i: (b, 0)),
                  pl.BlockSpec((width, seq), lambda b, i: (0, b))],
        out_specs=pl.BlockSpec((tq, width), lambda b, i: (b * nq + i, 0)),
        out_shape=jax.ShapeDtypeStruct((n, width), _BF16),
        compiler_params=_params(2),
        name="diff_attn",
    )(lam_p, qd, kd, vdt)


def _nsa_kernel(q_ref, gate_ref, ovt_ref, blk_ref, kc_ref, vct_ref, ks_ref, kw_ref, vt_ref, o_ref):
    i = pl.program_id(1)
    tq = NSA_TILE
    hpg = NSA_HPG
    groups = range(NSA_GROUPS)
    lane = lax.broadcasted_iota(jnp.int32, (1, LANES), 1)
    low = lane < HEAD_DIM
    q0 = i * tq

    def per_head(x):
        return jnp.concatenate([x] * hpg, axis=0)

    def group_cols(ref, rows, g):
        return ref[rows, g * LANES:(g + 1) * LANES]

    def stack_q(g):
        parts = []
        for j in range(hpg):
            c0 = g * hpg * HEAD_DIM + (j // 2) * LANES
            qc = q_ref[:, c0:c0 + LANES]
            keep = low if j % 2 == 0 else jnp.logical_not(low)
            parts.append(jnp.where(keep, qc, jnp.zeros_like(qc)))
        return jnp.concatenate(parts, axis=0)

    def per_head_cols(x):
        return jnp.concatenate([x] * hpg, axis=1)

    qpos_row = q0 + lax.broadcasted_iota(jnp.int32, (1, tq), 1)

    def compressed_and_select(g, qs):
        kc = kc_ref[0, 0, :, g * LANES:(g + 1) * LANES]
        vc_t = vct_ref[0, 0, g * HEAD_DIM:(g + 1) * HEAD_DIM, :]
        ncp = kc.shape[0]
        s_c = _dot_nt(kc, qs)
        cmp_end = lax.broadcasted_iota(jnp.int32, (ncp, 1), 0) * CMP_STRIDE + (CMP_BLOCK - 1)
        s_c = s_c + per_head_cols(_mask_bias(cmp_end <= qpos_row))
        mx = jnp.max(s_c, axis=0, keepdims=True)
        mx = jnp.where(mx == -jnp.inf, 0.0, mx)
        p_c = jnp.exp(s_c - mx)
        p_c = p_c * (1.0 / jnp.maximum(jnp.sum(p_c, axis=0, keepdims=True), 1e-30))
        o_c = _dot(vc_t, p_c.astype(_BF16))

        p_sum = p_c[:, 0:tq]
        for j in range(1, hpg):
            p_sum = p_sum + p_c[:, j * tq:(j + 1) * tq]
        ov_t = ovt_ref[...]
        n_blk = ov_t.shape[0]
        p_hi = p_sum.astype(_BF16)
        r1 = p_sum - p_hi.astype(_F32)
        p_mid = r1.astype(_BF16)
        p_lo = (r1 - p_mid.astype(_F32)).astype(_BF16)
        imp = _dot(ov_t, p_hi) + _dot(ov_t, p_mid) + _dot(ov_t, p_lo)

        jsel = lax.broadcasted_iota(jnp.int32, (n_blk, 1), 0)
        cur = qpos_row >> SEL_SHIFT
        forced = (jsel == 0) | (jsel == cur) | (jsel == cur - 1)
        future = jsel * SEL_BLOCK > qpos_row
        work = jnp.where(future, -1.0, jnp.where(forced, FORCED_SCORE, imp))

        jsel_f = jsel.astype(_F32)
        chosen = jnp.zeros((n_blk, tq), _F32)
        for _ in range(SEL_TOPK):
            best = jnp.max(work, axis=0, keepdims=True)
            first = jnp.min(jnp.where(work == best, jsel_f, float(SEL_PAD)), axis=0, keepdims=True)
            pick = jsel_f == first
            chosen = jnp.where(pick, 1.0, chosen)
            work = jnp.where(pick, -jnp.inf, work)
        if n_blk < SEL_PAD:
            chosen = jnp.concatenate([chosen, jnp.ones((SEL_PAD - n_blk, tq), _F32)], axis=0)
        return o_c, chosen

    qs, o_c, q_sel = [], [], []
    for g in groups:
        qs.append(stack_q(g))
        oc, chosen = compressed_and_select(g, qs[g])
        o_c.append(oc)
        penalty = ((chosen - 1.0) * MASKED_SCORE).T.astype(_BF16)
        q_sel.append(jnp.concatenate([qs[g], per_head(penalty)], axis=1))


    def v_rows(kind, g, keys):
        r0 = (kind * NSA_GROUPS + g) * HEAD_DIM
        ones_rows = jnp.ones((ONES_ROWS, keys.size), _BF16)
        return jnp.concatenate([vt_ref[r0:r0 + HEAD_DIM, keys], ones_rows], axis=0)

    def sel_scores(g, keys):
        k_blk = jnp.concatenate([group_cols(ks_ref, keys, g), blk_ref[keys, :]], axis=1)
        return _dot_nt(k_blk, q_sel[g])

    def sel_chunk(first_key, n_keys, carry, bias):
        keys = pl.ds(pl.multiple_of(first_key, tq), n_keys)
        s = [sel_scores(g, keys) for g in groups]
        if bias is not None:
            s = [x + bias for x in s]
        m_new = [jnp.max(x, axis=0, keepdims=True) for x in s]
        if carry is not None:
            m_new = [jnp.maximum(carry[g][0], m_new[g]) for g in groups]
        p = [jnp.exp((s[g] - m_new[g]).astype(_BF16)) for g in groups]
        res = [_dot(v_rows(0, g, keys), p[g]) for g in groups]
        if carry is not None:
            res = [jnp.exp(carry[g][0] - m_new[g]) * carry[g][1] + res[g] for g in groups]
        return tuple((m_new[g], res[g]) for g in groups)

    sub = tq // NSA_WINDOW_SPLIT
    wlen = WINDOW + sub
    key_w = lax.broadcasted_iota(jnp.int32, (wlen, sub), 0)
    qry_w = lax.broadcasted_iota(jnp.int32, (wlen, sub), 1)
    res_sub = []
    for t in range(NSA_WINDOW_SPLIT):
        w0 = jnp.maximum(q0 + t * sub - WINDOW, 0)
        wkeys = pl.ds(pl.multiple_of(w0, sub), wlen)
        qry = qry_w + (q0 + t * sub - w0)
        band = per_head_cols(
            jnp.where(key_w <= qry, jnp.where(key_w > qry - WINDOW, 0.0, -jnp.inf), -jnp.inf).astype(_F32))
        q_sub = [jnp.concatenate([qs[g][j * tq + t * sub:j * tq + (t + 1) * sub] for j in range(hpg)], axis=0)
                 for g in groups]
        s_w = [_dot_nt(group_cols(kw_ref, wkeys, g), q_sub[g]) + band for g in groups]
        p_w = [jnp.exp((x - jnp.max(x, axis=0, keepdims=True)).astype(_BF16)) for x in s_w]
        res_sub.append([_dot(v_rows(1, g, wkeys), p_w[g]) for g in groups])
    res_w = [jnp.concatenate([res_sub[t][g][:, j * sub:(j + 1) * sub]
                              for j in range(hpg) for t in range(NSA_WINDOW_SPLIT)], axis=1) for g in groups]

    key_t = lax.broadcasted_iota(jnp.int32, (tq, tq), 0)
    qry_t = lax.broadcasted_iota(jnp.int32, (tq, tq), 1)
    tri = per_head_cols(_mask_bias(key_t <= qry_t))
    carry = lax.cond(i % 2 == 1,
                     lambda: sel_chunk(q0 - tq, 2 * tq, None, jnp.concatenate([jnp.zeros_like(tri), tri], axis=0)),
                     lambda: sel_chunk(q0, tq, None, tri))
    carry = lax.fori_loop(0, i // 2, lambda j, c: sel_chunk(j * (2 * tq), 2 * tq, c, None), carry)

    def normalised_rows(res):
        return res[:HEAD_DIM] * (1.0 / res[HEAD_DIM:HEAD_DIM + 1])

    for g in groups:
        gate_t = gate_ref[:, g * LANES:(g + 1) * LANES].T
        o_s = normalised_rows(carry[g][1])
        o_w = normalised_rows(res_w[g])
        mixed = []
        for j in range(hpg):
            cols = slice(j * tq, (j + 1) * tq)
            mixed.append(gate_t[3 * j:3 * j + 1] * o_c[g][:, cols] + gate_t[3 * j + 1:3 * j + 2] * o_s[:, cols]
                         + gate_t[3 * j + 2:3 * j + 3] * o_w[:, cols])
        for c in range(hpg // 2):
            pair = jnp.concatenate([mixed[2 * c], mixed[2 * c + 1]], axis=0)
            col = (g * (hpg // 2) + c) * LANES
            o_ref[:, col:col + LANES] = pair.T.astype(o_ref.dtype)


def _overlap_matrix(seq):
    rows = seq // CMP_STRIDE
    n_cmp = (seq - CMP_BLOCK) // CMP_STRIDE + 1
    n_sel = seq // SEL_BLOCK
    cs = np.arange(rows)[:, None] * CMP_STRIDE
    ss = np.arange(SEL_PAD)[None, :] * SEL_BLOCK
    ov = np.clip(np.minimum(cs + CMP_BLOCK, ss + SEL_BLOCK) - np.maximum(cs, ss), 0, None) / CMP_BLOCK
    ov = ov * (np.arange(rows)[:, None] < n_cmp) * (np.arange(SEL_PAD)[None, :] < n_sel)
    n_blk = _sel_rows(seq)
    return jnp.asarray(ov.T[:n_blk], dtype=_BF16)


def _sel_rows(seq):
    n_sel = seq // SEL_BLOCK
    assert SEL_TOPK <= n_sel <= SEL_PAD
    return min(SEL_PAD, -(-n_sel // BF16_SUBLANES) * BF16_SUBLANES)


def _block_onehot(seq):
    hit = (np.arange(seq)[:, None] >> SEL_SHIFT) == np.arange(SEL_PAD)[None, :]
    return jnp.asarray(hit, dtype=_BF16)


def _nsa(qn, gn, ckv, ckv_t, ks, kw, vt, batch, seq):
    tq = NSA_TILE
    nq = seq // tq
    n = batch * seq
    rows = seq // CMP_STRIDE
    width = NSA_GROUPS * LANES
    assert seq // SEL_BLOCK <= SEL_PAD and seq >= WINDOW + tq
    assert tq % NSA_WINDOW_SPLIT == 0 and WINDOW % (tq // NSA_WINDOW_SPLIT) == 0
    assert _VT_DIFF_ROWS % _VT_NSA_ROWS == 0
    qmap = lambda b, i: (b * nq + i, 0)
    kvmap = lambda b, i: (b, 0)
    return pl.pallas_call(
        _nsa_kernel,
        grid=(batch, nq),
        in_specs=[pl.BlockSpec((tq, NSA_HEADS * HEAD_DIM), qmap),
                  pl.BlockSpec((tq, width), qmap),
                  _resident((_sel_rows(seq), rows), lambda b, i: (0, 0)),
                  _resident((seq, SEL_PAD), lambda b, i: (0, 0)),
                  pl.BlockSpec((1, 1, rows, width), lambda b, i: (0, b, 0, 0)),
                  pl.BlockSpec((1, 1, NSA_GROUPS * HEAD_DIM, rows), lambda b, i: (1, b, 0, 0)),
                  pl.BlockSpec((seq, width), kvmap), pl.BlockSpec((seq, width), kvmap),
                  pl.BlockSpec((_VT_NSA_ROWS, seq), lambda b, i: (_VT_DIFF_ROWS // _VT_NSA_ROWS, b))],
        out_specs=pl.BlockSpec((tq, NSA_HEADS * HEAD_DIM), qmap),
        out_shape=jax.ShapeDtypeStruct((n, NSA_HEADS * HEAD_DIM), _BF16),
        compiler_params=_params(2),
        name="nsa",
    )(qn, gn, _overlap_matrix(seq), _block_onehot(seq), ckv, ckv_t, ks, kw, vt)


def _dup_groups(w):
    a, b = w[:, :HEAD_DIM], w[:, HEAD_DIM:]
    return jnp.concatenate([a, a, b, b], axis=1)


def _prep_w_in(w):
    col = lambda k: w[:, _OFFS[k]:_OFFS[k + 1]]
    q_d, k_d, v_d, q_n, kc, vc, ks, vs, kw, vw, g_n, g_a, g_b = [col(k) for k in range(13)]
    per_group = NSA_HPG * 3
    pad = jnp.zeros((w.shape[0], LANES - per_group), w.dtype)
    gn = jnp.concatenate([g_n[:, :per_group], pad, g_n[:, per_group:], pad], axis=1)
    out = jnp.concatenate([q_d, k_d, q_n, _dup_groups(ks), _dup_groups(kw), kc, vc, gn, g_a, g_b], axis=1)
    assert out.shape[1] == _C_TOTAL
    w_vt = jnp.concatenate([v_d, vs, vw], axis=1).T
    assert w_vt.shape[0] == _VT_DIFF_ROWS + _VT_NSA_ROWS
    return out.astype(_BF16), w_vt.astype(_BF16)


def _prep_compress(pos, w1, w2):
    half = CMP_BLOCK // 2
    w1 = w1.reshape(2, 2, half, HEAD_DIM, CMP_HIDDEN)
    zeros = jnp.zeros_like(w1)
    per_group = []
    for g in range(NSA_GROUPS):
        lanes = [zeros, zeros]
        lanes[g] = w1
        per_group.append(jnp.concatenate(lanes, axis=3))
    w1_x = jnp.stack(per_group, axis=1).reshape(2, NSA_GROUPS, 2, half * LANES, CMP_HIDDEN).astype(_BF16)
    pos = pos.reshape(2, 2, half, HEAD_DIM)
    pos_x = jnp.concatenate([pos, pos], axis=3).reshape(2, 2, 1, half * LANES).astype(_F32)
    w2_x = jnp.concatenate([w2, w2], axis=2).astype(_BF16)
    w2_t = jnp.swapaxes(w2, 1, 2).astype(_BF16)
    return pos_x, w1_x, w2_x, w2_t


def kernel(x, positions, ffn1_norm, ffn1_w_gu, ffn1_w_down, mix_norm, w_in, diff_lambda, cmp_pos, cmp_w1, cmp_w2, w_branch_a, w_branch_b, w_out, ffn2_norm, ffn2_w_gu, ffn2_w_down, final_norm):
    batch, seq, _ = x.shape
    n = batch * seq
    depth = w_in.shape[0]
    assert seq % max(NSA_TILE, DIFF_TILE, CMP_STRIDE * 8) == 0
    cos_t, sin_t = _rope_tables(positions)
    h = x.reshape(n, D_MODEL)
    for l in range(depth):
        bf = lambda a: a.astype(_BF16)
        h = _ffn(h, ffn1_norm[l], bf(ffn1_w_gu[l]), bf(ffn1_w_down[l]))
        (qd, kd, qn, ks, kw, vt, craw, gn, ga, gb) = _mix_in(h, mix_norm[l], cos_t, sin_t, *_prep_w_in(w_in[l]))
        ckv, ckv_t = _compress(craw, *_prep_compress(cmp_pos[l], cmp_w1[l], cmp_w2[l]), batch, seq)
        lam_init = 0.8 - 0.6 * math.exp(-0.3 * l)
        o_a = _diff_attn(diff_lambda[l].astype(_F32), qd, kd, vt, batch, seq, lam_init)
        o_b = _nsa(qn, gn, ckv, ckv_t, ks, kw, vt, batch, seq)
        last = l == depth - 1
        h = _ffn(h, ffn2_norm[l], bf(ffn2_w_gu[l]), bf(ffn2_w_down[l]), final_g=final_norm if last else None,
                 mixers=(o_a, o_b, ga, gb, bf(w_branch_a[l]), bf(w_branch_b[l]), bf(w_out[l])))
    return h.reshape(batch, seq, D_MODEL)
```
